```python
import jax, jax.numpy as jnp
from jax import lax
import numpy as np


D_MODEL = 1024
BATCH = 1
SEQ = 16384
DEPTH = 2

CHUNK = 64
N_A_LAYERS = DEPTH // 2
N_B_LAYERS = DEPTH - N_A_LAYERS
EPS = 1e-6

GLA_HEADS = 4
GLA_DK = D_MODEL // 2
GLA_DV = D_MODEL
GLA_HK = GLA_DK // GLA_HEADS
GLA_HV = GLA_DV // GLA_HEADS
GLA_GATE_RANK = 16
GLA_TAU = 16.0
GLA_IN = 2 * GLA_DK + 2 * GLA_DV + GLA_GATE_RANK

MLA_HEADS = 16
MLA_NOPE = 64
MLA_ROPE = 32
MLA_QK = MLA_NOPE + MLA_ROPE
MLA_V = 64
MLA_Q_RANK = 384
MLA_KV_RANK = 256
ROPE_THETA = 10000.0
Q_BLOCK = 128

FFN_HIDDEN = -(-8 * D_MODEL // (3 * 256)) * 256

kernel_name = 'yoco_gla_mla_hybrid'


def rms_norm(x, g):
    xf = x.astype(jnp.float32)
    y = xf * lax.rsqrt(jnp.mean(xf * xf, axis=-1, keepdims=True) + EPS)
    return (y * g.astype(jnp.float32)).astype(x.dtype)


def rope(x, positions):
    half = x.shape[-1] // 2
    inv_freq = ROPE_THETA ** (-jnp.arange(half, dtype=jnp.float32) / half)
    ang = positions.astype(jnp.float32)[:, None, :, None] * inv_freq
    cos, sin = jnp.cos(ang), jnp.sin(ang)
    xf = x.astype(jnp.float32)
    x1, x2 = xf[..., :half], xf[..., half:]
    return jnp.concatenate([x1 * cos - x2 * sin, x1 * sin + x2 * cos], axis=-1).astype(x.dtype)


def gla_mixer(h, w_in, w_gate_up, b_gate, out_norm, w_out):
    B, S, _ = h.shape
    N = S // CHUNK
    f32 = jnp.float32
    proj = h @ w_in
    q = proj[..., :GLA_DK]
    k = proj[..., GLA_DK:2 * GLA_DK]
    v = proj[..., 2 * GLA_DK:2 * GLA_DK + GLA_DV]
    r = proj[..., 2 * GLA_DK + GLA_DV:2 * GLA_DK + 2 * GLA_DV]
    gl = proj[..., 2 * GLA_DK + 2 * GLA_DV:]
    log_a = jax.nn.log_sigmoid((gl @ w_gate_up + b_gate).astype(f32)) / GLA_TAU

    def chunks(t, d):
        return t.astype(f32).reshape(B, N, CHUNK, GLA_HEADS, d).transpose(0, 3, 1, 2, 4)

    qc = chunks(q, GLA_HK) * (GLA_HK ** -0.5)
    kc = chunks(k, GLA_HK)
    vc = chunks(v, GLA_HV)
    b = jnp.cumsum(chunks(log_a, GLA_HK), axis=3)
    b_last = b[:, :, :, -1:, :]
    q_dec = qc * jnp.exp(b)
    k_inv = kc * jnp.exp(-b)
    k_end = kc * jnp.exp(b_last - b)
    causal = jnp.tril(jnp.ones((CHUNK, CHUNK), dtype=bool))
    attn = jnp.where(causal, jnp.einsum('bhnik,bhnjk->bhnij', q_dec, k_inv), 0.0)
    o_intra = jnp.einsum('bhnij,bhnjv->bhniv', attn, vc)
    kv_chunk = jnp.einsum('bhnjk,bhnjv->nbhkv', k_end, vc)
    decay = jnp.exp(b_last[:, :, :, 0, :]).transpose(2, 0, 1, 3)

    def step(state, inp):
        dec, kv = inp
        return dec[..., None] * state + kv, state

    _, s_prev = lax.scan(step, jnp.zeros((B, GLA_HEADS, GLA_HK, GLA_HV), f32), (decay, kv_chunk))
    o_inter = jnp.einsum('bhnik,nbhkv->bhniv', q_dec, s_prev)
    o = (o_intra + o_inter).transpose(0, 2, 3, 1, 4).reshape(B, S, GLA_HEADS, GLA_HV)
    o = rms_norm(o, out_norm).reshape(B, S, GLA_DV) * jax.nn.silu(r.astype(f32))
    return o.astype(h.dtype) @ w_out


def mla_shared_kv(h, positions, kv_norm, w_down, latent_norm, w_up, k_norm):
    B, S, _ = h.shape
    d = rms_norm(h, kv_norm) @ w_down
    c_kv = rms_norm(d[..., :MLA_KV_RANK], latent_norm)
    k_rope = d[..., MLA_KV_RANK:]
    kv = (c_kv @ w_up).reshape(B, S, MLA_HEADS, MLA_NOPE + MLA_V)
    k_nope, v = kv[..., :MLA_NOPE], kv[..., MLA_NOPE:]
    k = jnp.concatenate([k_nope, jnp.broadcast_to(k_rope[:, :, None, :], (B, S, MLA_HEADS, MLA_ROPE))], axis=-1)
    k = rms_norm(k, k_norm).transpose(0, 2, 1, 3)
    k = jnp.concatenate([k[..., :MLA_NOPE], rope(k[..., MLA_NOPE:], positions)], axis=-1)
    return k, v.transpose(0, 2, 1, 3)


def mla_mixer(h, positions, k, v, w_dq, q_latent_norm, w_uq, q_norm, w_out):
    B, S, _ = h.shape
    f32 = jnp.float32
    c_q = rms_norm(h @ w_dq, q_latent_norm)
    q = (c_q @ w_uq).reshape(B, S, MLA_HEADS, MLA_QK)
    q = rms_norm(q, q_norm).transpose(0, 2, 1, 3)
    q = jnp.concatenate([q[..., :MLA_NOPE], rope(q[..., MLA_NOPE:], positions)], axis=-1)
    nb = S // Q_BLOCK
    q_blocks = q.astype(f32).reshape(B, MLA_HEADS, nb, Q_BLOCK, MLA_QK).transpose(2, 0, 1, 3, 4)
    kf, vf = k.astype(f32), v.astype(f32)
    key_chunk = jnp.arange(S) // CHUNK
    scale = MLA_QK ** -0.5

    def attend(args):
        qb, blk = args
        q_chunk = (blk * Q_BLOCK + jnp.arange(Q_BLOCK)) // CHUNK
        s = jnp.einsum('bhqd,bhkd->bhqk', qb, kf) * scale
        s = jnp.where(key_chunk[None, :] <= q_chunk[:, None], s, -jnp.inf)
        p = jax.nn.softmax(s, axis=-1)
        return jnp.einsum('bhqk,bhkv->bhqv', p, vf)

    o = lax.map(attend, (q_blocks, jnp.arange(nb)))
    o = o.transpose(1, 0, 3, 2, 4).reshape(B, S, MLA_HEADS * MLA_V)
    return o.astype(h.dtype) @ w_out


def swiglu(h, w_in, w_out):
    gu = h @ w_in
    return (jax.nn.silu(gu[..., :FFN_HIDDEN]) * gu[..., FFN_HIDDEN:]) @ w_out


def setup_inputs(seed: int = 0) -> dict:
    key = jax.random.key(seed)
    ks = jax.random.split(key, 24)
    f32 = jnp.float32
    na, nb = N_A_LAYERS, N_B_LAYERS
    out_scale = (2.0 * DEPTH) ** -0.5

    def w(k, shape, fan_in, scale=1.0):
        return jax.random.normal(k, shape, f32) * (scale * fan_in ** -0.5)

    def gain(k, shape):
        return 1.0 + 0.02 * jax.random.normal(k, shape, f32)

    x = jax.random.normal(ks[0], (BATCH, SEQ, D_MODEL), f32)
    start = jax.random.randint(ks[1], (BATCH, 1), 0, 4096, dtype=jnp.int32)
    positions = start + jnp.arange(SEQ, dtype=jnp.int32)[None, :]
    return {
        'x': x,
        'positions': positions,
        'a_norm': gain(ks[2], (na, D_MODEL)),
        'a_w_in': w(ks[3], (na, D_MODEL, GLA_IN), D_MODEL),
        'a_w_gate_up': w(ks[4], (na, GLA_GATE_RANK, GLA_DK), GLA_GATE_RANK),
        'a_b_gate': 0.1 * jax.random.normal(ks[5], (na, GLA_DK), f32),
        'a_out_norm': gain(ks[6], (na, GLA_HV)),
        'a_w_out': w(ks[7], (na, GLA_DV, D_MODEL), GLA_DV, out_scale),
        'b_norm': gain(ks[8], (nb, D_MODEL)),
        'b_w_dq': w(ks[9], (nb, D_MODEL, MLA_Q_RANK), D_MODEL),
        'b_q_latent_norm': gain(ks[10], (nb, MLA_Q_RANK)),
        'b_w_uq': w(ks[11], (nb, MLA_Q_RANK, MLA_HEADS * MLA_QK), MLA_Q_RANK),
        'b_q_norm': gain(ks[12], (nb, MLA_QK)),
        'b_w_out': w(ks[13], (nb, MLA_HEADS * MLA_V, D_MODEL), MLA_HEADS * MLA_V, out_scale),
        'kv_norm': gain(ks[14], (D_MODEL,)),
        'kv_w_down': w(ks[15], (D_MODEL, MLA_KV_RANK + MLA_ROPE), D_MODEL),
        'kv_latent_norm': gain(ks[16], (MLA_KV_RANK,)),
        'kv_w_up': w(ks[17], (MLA_KV_RANK, MLA_HEADS * (MLA_NOPE + MLA_V)), MLA_KV_RANK),
        'k_norm': gain(ks[18], (MLA_QK,)),
        'f_norm': gain(ks[19], (DEPTH, D_MODEL)),
        'f_w_in': w(ks[20], (DEPTH, D_MODEL, 2 * FFN_HIDDEN), D_MODEL),
        'f_w_out': w(ks[21], (DEPTH, FFN_HIDDEN, D_MODEL), FFN_HIDDEN, out_scale),
    }


def reference(x, positions, a_norm, a_w_in, a_w_gate_up, a_b_gate, a_out_norm, a_w_out,
              b_norm, b_w_dq, b_q_latent_norm, b_w_uq, b_q_norm, b_w_out,
              kv_norm, kv_w_down, kv_latent_norm, kv_w_up, k_norm,
              f_norm, f_w_in, f_w_out):
    k_sh, v_sh = None, None
    for layer in range(DEPTH):
        if layer < N_A_LAYERS:
            i = layer
            x = x + gla_mixer(rms_norm(x, a_norm[i]), a_w_in[i], a_w_gate_up[i], a_b_gate[i],
                              a_out_norm[i], a_w_out[i])
        else:
            j = layer - N_A_LAYERS
            x = x + mla_mixer(rms_norm(x, b_norm[j]), positions, k_sh, v_sh, b_w_dq[j],
                              b_q_latent_norm[j], b_w_uq[j], b_q_norm[j], b_w_out[j])
        x = x + swiglu(rms_norm(x, f_norm[layer]), f_w_in[layer], f_w_out[layer])
        if layer == N_A_LAYERS - 1 and N_B_LAYERS > 0:
            k_sh, v_sh = mla_shared_kv(x, positions, kv_norm, kv_w_down, kv_latent_norm, kv_w_up, k_norm)
    return x
```

```python
import functools

import jax
import jax.numpy as jnp
from jax import lax
from jax.experimental import pallas as pl
from jax.experimental.pallas import tpu as pltpu

F32 = jnp.float32
BF16 = jnp.bfloat16

EPS = 1e-6
CHUNK = 64
GLA_HEADS = 4
GLA_TAU = 16.0
GLA_GATE_RANK = 16
MLA_HEADS = 16
MLA_NOPE = 64
MLA_ROPE = 32
MLA_QK = MLA_NOPE + MLA_ROPE
MLA_V = 64
MLA_KV_RANK = 256
ROPE_THETA = 10000.0

LANES = 128
QK_PAD = LANES
V7X_VMEM_LIMIT_BYTES = 56 * 1024 * 1024

GLA_ROWS = 256
FFN_ROWS = 256
PROJ_ROWS = 256
ATTN_TQ = 512
ATTN_TK = 256


def _dot(a, b):
    return jnp.dot(a, b, preferred_element_type=F32)


def _dot_nt(a, b):
    return lax.dot_general(a, b, (((1,), (1,)), ((), ())), preferred_element_type=F32)


def _dot_tn(a, b):
    return lax.dot_general(a, b, (((0,), (0,)), ((), ())), preferred_element_type=F32)


def _rms_rows(x, g):
    return x * lax.rsqrt(jnp.mean(x * x, axis=-1, keepdims=True) + EPS) * g


def _silu(x):
    return x * jax.nn.sigmoid(x)


def _params(semantics):
    return pltpu.CompilerParams(dimension_semantics=semantics,
                                vmem_limit_bytes=V7X_VMEM_LIMIT_BYTES)


def _full(shape):
    return pl.BlockSpec(shape, lambda *_: (0,) * len(shape))


def _gla_kernel(x_ref, norm_ref, w_main_ref, w_gl_ref, w_gate_ref, b_gate_ref, onorm_ref,
                w_out_ref, out_ref, state_ref, proj_ref, b_ref, o_ref, *, rows, dk, dv):
    hk, hv = dk // GLA_HEADS, dv // GLA_HEADS

    @pl.when(pl.program_id(0) == 0)
    def _():
        state_ref[...] = jnp.zeros_like(state_ref)

    x = x_ref[...]
    h = _rms_rows(x, norm_ref[...]).astype(BF16)
    proj_ref[...] = _dot(h, w_main_ref[...])
    gl = _dot(h, w_gl_ref[...]).astype(BF16)
    z = _dot(gl, w_gate_ref[...]) + b_gate_ref[...]
    log_a = (jnp.minimum(z, 0.0) - jnp.log1p(jnp.exp(-jnp.abs(z)))) * (1.0 / GLA_TAU)

    r_i = lax.broadcasted_iota(jnp.int32, (rows, rows), 0)
    c_i = lax.broadcasted_iota(jnp.int32, (rows, rows), 1)
    tri = ((c_i <= r_i) & ((r_i >> 6) == (c_i >> 6))).astype(F32)
    b_ref[...] = jnp.dot(tri, log_a, preferred_element_type=F32, precision=lax.Precision.HIGHEST)

    qi = lax.broadcasted_iota(jnp.int32, (CHUNK, CHUNK), 0)
    kj = lax.broadcasted_iota(jnp.int32, (CHUNK, CHUNK), 1)
    causal = kj <= qi
    q_scale = hk ** -0.5

    for c in range(rows // CHUNK):
        r0 = c * CHUNK
        b_c = b_ref[r0:r0 + CHUNK, :]
        b_last = b_c[CHUNK - 1:CHUNK, :]
        q_c = proj_ref[r0:r0 + CHUNK, 0:dk]
        k_c = proj_ref[r0:r0 + CHUNK, dk:2 * dk]
        v_c = proj_ref[r0:r0 + CHUNK, 2 * dk:2 * dk + dv].astype(BF16)
        q_dec = (q_c * q_scale * jnp.exp(b_c)).astype(BF16)
        k_inv = (k_c * jnp.exp(-b_c)).astype(BF16)
        k_end = (k_c * jnp.exp(b_last - b_c)).astype(BF16)
        decay = jnp.exp(b_last)
        for hh in range(GLA_HEADS):
            ks = slice(hh * hk, (hh + 1) * hk)
            vs = slice(hh * hv, (hh + 1) * hv)
            attn = jnp.where(causal, _dot_nt(q_dec[:, ks], k_inv[:, ks]), 0.0).astype(BF16)
            state_t = state_ref[hh]
            o_h = _dot(attn, v_c[:, vs]) + _dot_nt(q_dec[:, ks], state_t.astype(BF16))
            state_ref[hh] = decay[:, ks] * state_t + _dot_tn(v_c[:, vs], k_end[:, ks])
            o_ref[r0:r0 + CHUNK, vs] = _rms_rows(o_h, onorm_ref[...])

    gate = _silu(proj_ref[:, 2 * dk + dv:2 * dk + 2 * dv])
    out_ref[...] = x + _dot((o_ref[...] * gate).astype(BF16), w_out_ref[...])


def _gla_layer(x, norm, w_in, w_gate_up, b_gate, out_norm, w_out):
    s, d = x.shape
    dk = w_gate_up.shape[1]
    dv = (w_in.shape[1] - 2 * dk - GLA_GATE_RANK) // 2
    n_main = 2 * dk + 2 * dv
    rows = GLA_ROWS
    w_main = w_in[:, :n_main].astype(BF16)
    w_gl = jnp.pad(w_in[:, n_main:], ((0, 0), (0, LANES - GLA_GATE_RANK))).astype(BF16)
    w_gate = jnp.pad(w_gate_up, ((0, LANES - GLA_GATE_RANK), (0, 0))).astype(BF16)
    kern = functools.partial(_gla_kernel, rows=rows, dk=dk, dv=dv)
    return pl.pallas_call(
        kern,
        out_shape=jax.ShapeDtypeStruct((s, d), F32),
        grid=(s // rows,),
        in_specs=[
            pl.BlockSpec((rows, d), lambda i: (i, 0)),
            _full((1, d)),
            _full((d, n_main)),
            _full((d, LANES)),
            _full((LANES, dk)),
            _full((1, dk)),
            _full((1, dv // GLA_HEADS)),
            _full((dv, d)),
        ],
        out_specs=pl.BlockSpec((rows, d), lambda i: (i, 0)),
        scratch_shapes=[
            pltpu.VMEM((GLA_HEADS, dv // GLA_HEADS, dk // GLA_HEADS), F32),
            pltpu.VMEM((rows, n_main), F32),
            pltpu.VMEM((rows, dk), F32),
            pltpu.VMEM((rows, dv), F32),
        ],
        compiler_params=_params(("arbitrary",)),
        name="gla_layer",
    )(x, norm.reshape(1, d), w_main, w_gl, w_gate, b_gate.reshape(1, dk),
      out_norm.reshape(1, -1), w_out.astype(BF16))


def _ffn_body(x, norm_ref, w_in_ref, w_out_ref, out_ref, hidden):
    h = _rms_rows(x, norm_ref[...]).astype(BF16)
    gate = _dot(h, w_in_ref[:, 0:hidden])
    up = _dot(h, w_in_ref[:, hidden:2 * hidden])
    act = (_silu(gate) * up).astype(BF16)
    out_ref[...] = x + _dot(act, w_out_ref[...])


def _ffn_kernel(x_ref, norm_ref, w_in_ref, w_out_ref, out_ref, *, hidden):
    _ffn_body(x_ref[...], norm_ref, w_in_ref, w_out_ref, out_ref, hidden)


def _proj_ffn_kernel(x_ref, ot_ref, w_o_ref, norm_ref, w_in_ref, w_out_ref, out_ref, *, hidden):
    x = x_ref[...] + _dot_tn(ot_ref[...], w_o_ref[...])
    _ffn_body(x, norm_ref, w_in_ref, w_out_ref, out_ref, hidden)


def _ffn(x, norm, w_in, w_out, attn_t=None, w_o=None):
    s, d = x.shape
    hidden = w_out.shape[0]
    rows = FFN_ROWS
    row_spec = pl.BlockSpec((rows, d), lambda i: (i, 0))
    ffn_specs = [_full((1, d)), _full((d, 2 * hidden)), _full((hidden, d))]
    ffn_args = (norm.reshape(1, d), w_in.astype(BF16), w_out.astype(BF16))
    if attn_t is None:
        kern = functools.partial(_ffn_kernel, hidden=hidden)
        in_specs = [row_spec] + ffn_specs
        args = (x,) + ffn_args
        name = "ffn"
    else:
        kern = functools.partial(_proj_ffn_kernel, hidden=hidden)
        in_specs = [row_spec, pl.BlockSpec((attn_t.shape[0], rows), lambda i: (0, i)),
                    _full(w_o.shape)] + ffn_specs
        args = (x, attn_t, w_o.astype(BF16)) + ffn_args
        name = "attn_proj_ffn"
    return pl.pallas_call(
        kern,
        out_shape=jax.ShapeDtypeStruct((s, d), F32),
        grid=(s // rows,),
        in_specs=in_specs,
        out_specs=row_spec,
        compiler_params=_params(("parallel",)),
        name=name,
    )(*args)


def _rope_tables(pos_ref, inv_freq_ref):
    ang = pos_ref[...].astype(F32) * inv_freq_ref[...]
    return jnp.cos(ang), jnp.sin(ang)


def _kv_kernel(x_ref, norm_ref, w_dc_ref, w_drt_ref, lnorm_ref, w_upt_ref, knorm_ref, pos_ref,
               inv_freq_ref, k_ref, vt_ref, kvt_ref, *, rows):
    half = MLA_ROPE // 2
    h = _rms_rows(x_ref[...], norm_ref[...]).astype(BF16)
    c_kv = _rms_rows(_dot(h, w_dc_ref[...]), lnorm_ref[...]).astype(BF16)
    kvt_ref[...] = _dot_nt(w_upt_ref[...], c_kv)
    kr = _dot_nt(w_drt_ref[...], h)
    cos, sin = _rope_tables(pos_ref, inv_freq_ref)
    g = knorm_ref[...]
    ss_rope = jnp.sum(kr * kr, axis=0, keepdims=True)
    krg = kr * g[MLA_NOPE:, :]
    x1, x2 = krg[:half, :], krg[half:, :]
    r1 = x1 * cos - x2 * sin
    r2 = x1 * sin + x2 * cos
    pad = jnp.zeros((QK_PAD - MLA_QK, rows), F32)
    per_head = MLA_NOPE + MLA_V
    for hh in range(MLA_HEADS):
        kn = kvt_ref[hh * per_head:hh * per_head + MLA_NOPE, :]
        ss = jnp.sum(kn * kn, axis=0, keepdims=True) + ss_rope
        inv = lax.rsqrt(ss * (1.0 / MLA_QK) + EPS)
        kt = jnp.concatenate([kn * inv * g[:MLA_NOPE, :], r1 * inv, r2 * inv, pad], axis=0)
        k_ref[hh] = kt.T.astype(BF16)
        vt_ref[hh, 0] = kvt_ref[hh * per_head + MLA_NOPE:(hh + 1) * per_head, :].astype(BF16)


def _mla_kv(x, pos, inv_freq, kv_norm, w_down, latent_norm, w_up, k_norm):
    s, d = x.shape
    rows = PROJ_ROWS
    w_dc = w_down[:, :MLA_KV_RANK].astype(BF16)
    w_drt = w_down[:, MLA_KV_RANK:].T.astype(BF16)
    w_upt = w_up.T.astype(BF16)
    n_up = w_upt.shape[0]
    kern = functools.partial(_kv_kernel, rows=rows)
    return pl.pallas_call(
        kern,
        out_shape=(jax.ShapeDtypeStruct((MLA_HEADS, s, QK_PAD), BF16),
                   jax.ShapeDtypeStruct((MLA_HEADS, s // rows, MLA_V, rows), BF16)),
        grid=(s // rows,),
        in_specs=[
            pl.BlockSpec((rows, d), lambda i: (i, 0)),
            _full((1, d)),
            _full((d, MLA_KV_RANK)),
            _full((MLA_ROPE, d)),
            _full((1, MLA_KV_RANK)),
            _full((n_up, MLA_KV_RANK)),
            _full((MLA_QK, 1)),
            pl.BlockSpec((1, rows), lambda i: (0, i)),
            _full((MLA_ROPE // 2, 1)),
        ],
        out_specs=(pl.BlockSpec((MLA_HEADS, rows, QK_PAD), lambda i: (0, i, 0)),
                   pl.BlockSpec((MLA_HEADS, 1, MLA_V, rows), lambda i: (0, i, 0, 0))),
        scratch_shapes=[pltpu.VMEM((n_up, rows), F32)],
        compiler_params=_params(("parallel",)),
        name="mla_kv",
    )(x, kv_norm.reshape(1, d), w_dc, w_drt, latent_norm.reshape(1, -1), w_upt,
      k_norm.reshape(-1, 1), pos, inv_freq)


def _q_kernel(x_ref, norm_ref, w_dq_ref, qlnorm_ref, w_uqt_ref, qnorm_ref, pos_ref, inv_freq_ref,
              qt_ref, qall_ref, *, rows):
    half = MLA_ROPE // 2
    h = _rms_rows(x_ref[...], norm_ref[...]).astype(BF16)
    c_q = _rms_rows(_dot(h, w_dq_ref[...]), qlnorm_ref[...]).astype(BF16)
    qall_ref[...] = _dot_nt(w_uqt_ref[...], c_q)
    cos, sin = _rope_tables(pos_ref, inv_freq_ref)
    g = qnorm_ref[...]
    scale = MLA_QK ** -0.5
    pad = jnp.zeros((QK_PAD - MLA_QK, rows), F32)
    for hh in range(MLA_HEADS):
        q = qall_ref[hh * MLA_QK:(hh + 1) * MLA_QK, :]
        inv = lax.rsqrt(jnp.mean(q * q, axis=0, keepdims=True) + EPS)
        qn = q * inv * g
        x1, x2 = qn[MLA_NOPE:MLA_NOPE + half, :], qn[MLA_NOPE + half:, :]
        qt = jnp.concatenate([qn[:MLA_NOPE, :], x1 * cos - x2 * sin, x1 * sin + x2 * cos, pad],
                             axis=0)
        qt_ref[hh] = (qt * scale).astype(BF16)


def _mla_q(x, pos, inv_freq, norm, w_dq, q_latent_norm, w_uq, q_norm):
    s, d = x.shape
    rows = PROJ_ROWS
    rank = w_dq.shape[1]
    w_uqt = w_uq.T.astype(BF16)
    kern = functools.partial(_q_kernel, rows=rows)
    return pl.pallas_call(
        kern,
        out_shape=jax.ShapeDtypeStruct((MLA_HEADS, QK_PAD, s), BF16),
        grid=(s // rows,),
        in_specs=[
            pl.BlockSpec((rows, d), lambda i: (i, 0)),
            _full((1, d)),
            _full((d, rank)),
            _full((1, rank)),
            _full(w_uqt.shape),
            _full((MLA_QK, 1)),
            pl.BlockSpec((1, rows), lambda i: (0, i)),
            _full((MLA_ROPE // 2, 1)),
        ],
        out_specs=pl.BlockSpec((MLA_HEADS, QK_PAD, rows), lambda i: (0, 0, i)),
        scratch_shapes=[pltpu.VMEM(w_uqt.shape[:1] + (rows,), F32)],
        compiler_params=_params(("parallel",)),
        name="mla_q",
    )(x, norm.reshape(1, d), w_dq.astype(BF16), q_latent_norm.reshape(1, -1), w_uqt,
      q_norm.reshape(-1, 1), pos, inv_freq)


def _attn_kernel(qt_ref, k_ref, vt_ref, o_ref, *, tq, tk):
    i = pl.program_id(1)
    qt = qt_ref[0]

    def step(k, vt, carry, mask):
        m, l, acc = carry
        s = _dot(k, qt)
        if mask is not None:
            s = jnp.where(mask, s, -jnp.inf)
        m_new = jnp.maximum(m, jnp.max(s, axis=0, keepdims=True))
        alpha = jnp.exp(m - m_new)
        p = jnp.exp(s - m_new)
        l = alpha * l + jnp.sum(p, axis=0, keepdims=True)
        acc = alpha * acc + _dot(vt, p.astype(BF16))
        return m_new, l, acc

    def full_tile(j, carry):
        k0 = pl.multiple_of(j * tk, tk)
        return step(k_ref[0, pl.ds(k0, tk), :], vt_ref[0, j], carry, None)

    per_q = tq // tk
    carry = (jnp.full((1, tq), -jnp.inf, F32), jnp.zeros((1, tq), F32),
             jnp.zeros((MLA_V, tq), F32))
    carry = lax.fori_loop(0, i * per_q, full_tile, carry)

    key_c = lax.broadcasted_iota(jnp.int32, (tk, tq), 0) >> 6
    qry_c = lax.broadcasted_iota(jnp.int32, (tk, tq), 1) >> 6
    for jj in range(per_q):
        j = i * per_q + jj
        k0 = pl.multiple_of(j * tk, tk)
        mask = (key_c + jj * (tk // CHUNK)) <= qry_c
        carry = step(k_ref[0, pl.ds(k0, tk), :], vt_ref[0, j], carry, mask)

    _, l, acc = carry
    o_ref[...] = (acc / l).astype(BF16)


def _mla_attention(qt, k, vt):
    heads, _, s = qt.shape
    tq, tk = ATTN_TQ, ATTN_TK
    kern = functools.partial(_attn_kernel, tq=tq, tk=tk)
    return pl.pallas_call(
        kern,
        out_shape=jax.ShapeDtypeStruct((heads * MLA_V, s), BF16),
        grid=(heads, s // tq),
        in_specs=[
            pl.BlockSpec((1, QK_PAD, tq), lambda h, i: (h, 0, i)),
            pl.BlockSpec((1, s, QK_PAD), lambda h, i: (h, 0, 0)),
            pl.BlockSpec((1, s // tk, MLA_V, tk), lambda h, i: (h, 0, 0, 0)),
        ],
        out_specs=pl.BlockSpec((MLA_V, tq), lambda h, i: (h, i)),
        compiler_params=_params(("parallel", "arbitrary")),
        name="mla_attn",
    )(qt, k, vt)


def kernel(x, positions, a_norm, a_w_in, a_w_gate_up, a_b_gate, a_out_norm, a_w_out, b_norm, b_w_dq,
           b_q_latent_norm, b_w_uq, b_q_norm, b_w_out, kv_norm, kv_w_down, kv_latent_norm, kv_w_up,
           k_norm, f_norm, f_w_in, f_w_out):
    batch, s, d = x.shape
    n_a = a_norm.shape[0]
    n_b = b_norm.shape[0]
    half = MLA_ROPE // 2
    inv_freq = (ROPE_THETA ** (-jnp.arange(half, dtype=F32) / half)).reshape(half, 1)
    outs = []
    for bi in range(batch):
        xb = x[bi]
        pos = positions[bi].reshape(1, s)
        k_sh = vt_sh = None
        for layer in range(n_a + n_b):
            if layer < n_a:
                xb = _gla_layer(xb, a_norm[layer], a_w_in[layer], a_w_gate_up[layer],
                                a_b_gate[layer], a_out_norm[layer], a_w_out[layer])
                xb = _ffn(xb, f_norm[layer], f_w_in[layer], f_w_out[layer])
            else:
                j = layer - n_a
                qt = _mla_q(xb, pos, inv_freq, b_norm[j], b_w_dq[j], b_q_latent_norm[j],
                            b_w_uq[j], b_q_norm[j])
                attn_t = _mla_attention(qt, k_sh, vt_sh)
                xb = _ffn(xb, f_norm[layer], f_w_in[layer], f_w_out[layer],
                          attn_t=attn_t, w_o=b_w_out[j])
            if layer == n_a - 1 and n_b > 0:
                k_sh, vt_sh = _mla_kv(xb, pos, inv_freq, kv_norm, kv_w_down, kv_latent_norm,
                                      kv_w_up, k_norm)
        outs.append(xb)
    return jnp.stack(outs, axis=0)
```

```python
import functools

import jax
import jax.numpy as jnp
from jax import lax
from jax.experimental import pallas as pl
from jax.experimental.pallas import tpu as pltpu

F32 = jnp.float32
BF16 = jnp.bfloat16

EPS = 1e-6
CHUNK = 64
GLA_HEADS = 4
GLA_TAU = 16.0
GLA_GATE_RANK = 16
MLA_HEADS = 16
MLA_NOPE = 64
MLA_ROPE = 32
MLA_QK = MLA_NOPE + MLA_ROPE
MLA_V = 64
MLA_KV_RANK = 256
ROPE_THETA = 10000.0
LOG2_E = 1.4426950408889634

LANES = 128
BF16_SUBLANES = 16
QK_PAD = LANES
V_AUG = MLA_V + BF16_SUBLANES
LOGIT_BOUND_COEF = 1.02 * LOG2_E * MLA_QK ** 0.5
MAX_BOUNDED_LOGIT = 100.0
V7X_VMEM_LIMIT_BYTES = 56 * 1024 * 1024

GLA_ROWS = 256
FFN_ROWS = 256
PROJ_ROWS = 256
ATTN_TQ = 1024
ATTN_TK = 256


def _dot(a, b):
    return jnp.dot(a, b, preferred_element_type=F32)


def _dot_nt(a, b):
    return lax.dot_general(a, b, (((1,), (1,)), ((), ())), preferred_element_type=F32)


def _dot_tn(a, b):
    return lax.dot_general(a, b, (((0,), (0,)), ((), ())), preferred_element_type=F32)


def _rms_rows(x, g):
    return x * lax.rsqrt(jnp.mean(x * x, axis=-1, keepdims=True) + EPS) * g


def _silu(x):
    return x * jax.nn.sigmoid(x)


def _params(semantics):
    return pltpu.CompilerParams(dimension_semantics=semantics,
                                vmem_limit_bytes=V7X_VMEM_LIMIT_BYTES)


def _full(shape):
    return pl.BlockSpec(shape, lambda *_: (0,) * len(shape))


def _gla_kernel(x_ref, norm_ref, w_main_ref, w_gl_ref, w_gate_ref, b_gate_ref, onorm_ref,
                w_out_ref, out_ref, state_ref, proj_ref, b_ref, o_ref, *, rows, dk, dv):
    hk, hv = dk // GLA_HEADS, dv // GLA_HEADS

    @pl.when(pl.program_id(0) == 0)
    def _():
        state_ref[...] = jnp.zeros_like(state_ref)

    x = x_ref[...]
    h = _rms_rows(x, norm_ref[...]).astype(BF16)
    proj_ref[...] = _dot(h, w_main_ref[...])
    gl = _dot(h, w_gl_ref[...]).astype(BF16)
    z = _dot(gl, w_gate_ref[...]) + b_gate_ref[...]
    log_a = (jnp.minimum(z, 0.0) - jnp.log1p(jnp.exp(-jnp.abs(z)))) * (1.0 / GLA_TAU)

    r_i = lax.broadcasted_iota(jnp.int32, (rows, rows), 0)
    c_i = lax.broadcasted_iota(jnp.int32, (rows, rows), 1)
    tri = ((c_i <= r_i) & ((r_i >> 6) == (c_i >> 6))).astype(F32)
    b_ref[...] = jnp.dot(tri, log_a, preferred_element_type=F32, precision=lax.Precision.HIGHEST)

    qi = lax.broadcasted_iota(jnp.int32, (CHUNK, CHUNK), 0)
    kj = lax.broadcasted_iota(jnp.int32, (CHUNK, CHUNK), 1)
    causal = kj <= qi
    q_scale = hk ** -0.5

    for c in range(rows // CHUNK):
        r0 = c * CHUNK
        b_c = b_ref[r0:r0 + CHUNK, :]
        b_last = b_c[CHUNK - 1:CHUNK, :]
        q_c = proj_ref[r0:r0 + CHUNK, 0:dk]
        k_c = proj_ref[r0:r0 + CHUNK, dk:2 * dk]
        v_c = proj_ref[r0:r0 + CHUNK, 2 * dk:2 * dk + dv].astype(BF16)
        q_dec = (q_c * q_scale * jnp.exp(b_c)).astype(BF16)
        k_inv = (k_c * jnp.exp(-b_c)).astype(BF16)
        k_end = (k_c * jnp.exp(b_last - b_c)).astype(BF16)
        decay = jnp.exp(b_last)
        for hh in range(GLA_HEADS):
            ks = slice(hh * hk, (hh + 1) * hk)
            vs = slice(hh * hv, (hh + 1) * hv)
            attn = jnp.where(causal, _dot_nt(q_dec[:, ks], k_inv[:, ks]), 0.0).astype(BF16)
            state_t = state_ref[hh]
            o_h = _dot(attn, v_c[:, vs]) + _dot_nt(q_dec[:, ks], state_t.astype(BF16))
            state_ref[hh] = decay[:, ks] * state_t + _dot_tn(v_c[:, vs], k_end[:, ks])
            o_ref[r0:r0 + CHUNK, vs] = _rms_rows(o_h, onorm_ref[...])

    gate = _silu(proj_ref[:, 2 * dk + dv:2 * dk + 2 * dv])
    out_ref[...] = x + _dot((o_ref[...] * gate).astype(BF16), w_out_ref[...])


def _gla_layer(x, norm, w_in, w_gate_up, b_gate, out_norm, w_out):
    s, d = x.shape
    dk = w_gate_up.shape[1]
    dv = (w_in.shape[1] - 2 * dk - GLA_GATE_RANK) // 2
    n_main = 2 * dk + 2 * dv
    rows = GLA_ROWS
    w_main = w_in[:, :n_main].astype(BF16)
    w_gl = jnp.pad(w_in[:, n_main:], ((0, 0), (0, LANES - GLA_GATE_RANK))).astype(BF16)
    w_gate = jnp.pad(w_gate_up, ((0, LANES - GLA_GATE_RANK), (0, 0))).astype(BF16)
    kern = functools.partial(_gla_kernel, rows=rows, dk=dk, dv=dv)
    return pl.pallas_call(
        kern,
        out_shape=jax.ShapeDtypeStruct((s, d), F32),
        grid=(s // rows,),
        in_specs=[
            pl.BlockSpec((rows, d), lambda i: (i, 0)),
            _full((1, d)),
            _full((d, n_main)),
            _full((d, LANES)),
            _full((LANES, dk)),
            _full((1, dk)),
            _full((1, dv // GLA_HEADS)),
            _full((dv, d)),
        ],
        out_specs=pl.BlockSpec((rows, d), lambda i: (i, 0)),
        scratch_shapes=[
            pltpu.VMEM((GLA_HEADS, dv // GLA_HEADS, dk // GLA_HEADS), F32),
            pltpu.VMEM((rows, n_main), F32),
            pltpu.VMEM((rows, dk), F32),
            pltpu.VMEM((rows, dv), F32),
        ],
        compiler_params=_params(("arbitrary",)),
        name="gla_layer",
    )(x, norm.reshape(1, d), w_main, w_gl, w_gate, b_gate.reshape(1, dk),
      out_norm.reshape(1, -1), w_out.astype(BF16))


def _ffn_body(x, norm_ref, w_in_ref, w_out_ref, out_ref, hidden):
    h = _rms_rows(x, norm_ref[...]).astype(BF16)
    gate = _dot(h, w_in_ref[:, 0:hidden])
    up = _dot(h, w_in_ref[:, hidden:2 * hidden])
    act = (_silu(gate) * up).astype(BF16)
    out_ref[...] = x + _dot(act, w_out_ref[...])


def _ffn_kernel(x_ref, norm_ref, w_in_ref, w_out_ref, out_ref, *, hidden):
    _ffn_body(x_ref[...], norm_ref, w_in_ref, w_out_ref, out_ref, hidden)


def _proj_ffn_kernel(x_ref, ot_ref, w_o_ref, norm_ref, w_in_ref, w_out_ref, out_ref, *, hidden):
    x = x_ref[...] + _dot_tn(ot_ref[...], w_o_ref[...])
    _ffn_body(x, norm_ref, w_in_ref, w_out_ref, out_ref, hidden)


def _ffn(x, norm, w_in, w_out, attn_t=None, w_o=None):
    s, d = x.shape
    hidden = w_out.shape[0]
    rows = FFN_ROWS
    row_spec = pl.BlockSpec((rows, d), lambda i: (i, 0))
    ffn_specs = [_full((1, d)), _full((d, 2 * hidden)), _full((hidden, d))]
    ffn_args = (norm.reshape(1, d), w_in.astype(BF16), w_out.astype(BF16))
    if attn_t is None:
        kern = functools.partial(_ffn_kernel, hidden=hidden)
        in_specs = [row_spec] + ffn_specs
        args = (x,) + ffn_args
        name = "ffn"
    else:
        kern = functools.partial(_proj_ffn_kernel, hidden=hidden)
        in_specs = [row_spec, pl.BlockSpec((attn_t.shape[0], rows), lambda i: (0, i)),
                    _full(w_o.shape)] + ffn_specs
        args = (x, attn_t, w_o.astype(BF16)) + ffn_args
        name = "attn_proj_ffn"
    return pl.pallas_call(
        kern,
        out_shape=jax.ShapeDtypeStruct((s, d), F32),
        grid=(s // rows,),
        in_specs=in_specs,
        out_specs=row_spec,
        compiler_params=_params(("parallel",)),
        name=name,
    )(*args)


def _rope_tables(pos_ref, inv_freq_ref):
    ang = pos_ref[...].astype(F32) * inv_freq_ref[...]
    return jnp.cos(ang), jnp.sin(ang)


def _kv_kernel(x_ref, norm_ref, w_dc_ref, w_drt_ref, lnorm_ref, w_upt_ref, knorm_ref, pos_ref,
               inv_freq_ref, k_ref, vt_ref, kvt_ref, *, rows):
    half = MLA_ROPE // 2
    h = _rms_rows(x_ref[...], norm_ref[...]).astype(BF16)
    c_kv = _rms_rows(_dot(h, w_dc_ref[...]), lnorm_ref[...]).astype(BF16)
    kvt_ref[...] = _dot_nt(w_upt_ref[...], c_kv)
    kr = _dot_nt(w_drt_ref[...], h)
    cos, sin = _rope_tables(pos_ref, inv_freq_ref)
    g = knorm_ref[...]
    ss_rope = jnp.sum(kr * kr, axis=0, keepdims=True)
    krg = kr * g[MLA_NOPE:, :]
    x1, x2 = krg[:half, :], krg[half:, :]
    r1 = x1 * cos - x2 * sin
    r2 = x1 * sin + x2 * cos
    pad = jnp.zeros((QK_PAD - MLA_QK, rows), F32)
    ones_row = (lax.broadcasted_iota(jnp.int32, (V_AUG - MLA_V, rows), 0) == 0).astype(F32)
    per_head = MLA_NOPE + MLA_V
    for hh in range(MLA_HEADS):
        kn = kvt_ref[hh * per_head:hh * per_head + MLA_NOPE, :]
        ss = jnp.sum(kn * kn, axis=0, keepdims=True) + ss_rope
        inv = lax.rsqrt(ss * (1.0 / MLA_QK) + EPS)
        kt = jnp.concatenate([kn * inv * g[:MLA_NOPE, :], r1 * inv, r2 * inv, pad], axis=0)
        k_ref[hh] = kt.T.astype(BF16)
        vt = kvt_ref[hh * per_head + MLA_NOPE:(hh + 1) * per_head, :]
        vt_ref[hh, 0] = jnp.concatenate([vt, ones_row], axis=0).astype(BF16)


def _mla_kv(x, pos, inv_freq, kv_norm, w_down, latent_norm, w_up, k_norm):
    s, d = x.shape
    rows = PROJ_ROWS
    w_dc = w_down[:, :MLA_KV_RANK].astype(BF16)
    w_drt = w_down[:, MLA_KV_RANK:].T.astype(BF16)
    w_upt = w_up.T.astype(BF16)
    n_up = w_upt.shape[0]
    kern = functools.partial(_kv_kernel, rows=rows)
    return pl.pallas_call(
        kern,
        out_shape=(jax.ShapeDtypeStruct((MLA_HEADS, s, QK_PAD), BF16),
                   jax.ShapeDtypeStruct((MLA_HEADS, s // rows, V_AUG, rows), BF16)),
        grid=(s // rows,),
        in_specs=[
            pl.BlockSpec((rows, d), lambda i: (i, 0)),
            _full((1, d)),
            _full((d, MLA_KV_RANK)),
            _full((MLA_ROPE, d)),
            _full((1, MLA_KV_RANK)),
            _full((n_up, MLA_KV_RANK)),
            _full((MLA_QK, 1)),
            pl.BlockSpec((1, rows), lambda i: (0, i)),
            _full((MLA_ROPE // 2, 1)),
        ],
        out_specs=(pl.BlockSpec((MLA_HEADS, rows, QK_PAD), lambda i: (0, i, 0)),
                   pl.BlockSpec((MLA_HEADS, 1, V_AUG, rows), lambda i: (0, i, 0, 0))),
        scratch_shapes=[pltpu.VMEM((n_up, rows), F32)],
        compiler_params=_params(("parallel",)),
        name="mla_kv",
    )(x, kv_norm.reshape(1, d), w_dc, w_drt, latent_norm.reshape(1, -1), w_upt,
      k_norm.reshape(-1, 1), pos, inv_freq)


def _q_kernel(x_ref, norm_ref, w_dq_ref, qlnorm_ref, w_uqt_ref, qnorm_ref, pos_ref, inv_freq_ref,
              qt_ref, qall_ref, *, rows):
    half = MLA_ROPE // 2
    h = _rms_rows(x_ref[...], norm_ref[...]).astype(BF16)
    c_q = _rms_rows(_dot(h, w_dq_ref[...]), qlnorm_ref[...]).astype(BF16)
    qall_ref[...] = _dot_nt(w_uqt_ref[...], c_q)
    cos, sin = _rope_tables(pos_ref, inv_freq_ref)
    g = qnorm_ref[...]
    scale = MLA_QK ** -0.5 * LOG2_E
    pad = jnp.zeros((QK_PAD - MLA_QK, rows), F32)
    for hh in range(MLA_HEADS):
        q = qall_ref[hh * MLA_QK:(hh + 1) * MLA_QK, :]
        inv = lax.rsqrt(jnp.mean(q * q, axis=0, keepdims=True) + EPS)
        qn = q * inv * g
        x1, x2 = qn[MLA_NOPE:MLA_NOPE + half, :], qn[MLA_NOPE + half:, :]
        qt = jnp.concatenate([qn[:MLA_NOPE, :], x1 * cos - x2 * sin, x1 * sin + x2 * cos, pad],
                             axis=0)
        qt_ref[hh] = (qt * scale).astype(BF16)


def _mla_q(x, pos, inv_freq, norm, w_dq, q_latent_norm, w_uq, q_norm):
    s, d = x.shape
    rows = PROJ_ROWS
    rank = w_dq.shape[1]
    w_uqt = w_uq.T.astype(BF16)
    kern = functools.partial(_q_kernel, rows=rows)
    return pl.pallas_call(
        kern,
        out_shape=jax.ShapeDtypeStruct((MLA_HEADS, QK_PAD, s), BF16),
        grid=(s // rows,),
        in_specs=[
            pl.BlockSpec((rows, d), lambda i: (i, 0)),
            _full((1, d)),
            _full((d, rank)),
            _full((1, rank)),
            _full(w_uqt.shape),
            _full((MLA_QK, 1)),
            pl.BlockSpec((1, rows), lambda i: (0, i)),
            _full((MLA_ROPE // 2, 1)),
        ],
        out_specs=pl.BlockSpec((MLA_HEADS, QK_PAD, rows), lambda i: (0, 0, i)),
        scratch_shapes=[pltpu.VMEM(w_uqt.shape[:1] + (rows,), F32)],
        compiler_params=_params(("parallel",)),
        name="mla_q",
    )(x, norm.reshape(1, d), w_dq.astype(BF16), q_latent_norm.reshape(1, -1), w_uqt,
      q_norm.reshape(-1, 1), pos, inv_freq)


def _attn_kernel(bounded_ref, qt_ref, k_ref, vt_ref, o_ref, acc_ref, *, tq, tk):
    i = pl.program_id(1)
    per_q = tq // tk
    key_c = lax.broadcasted_iota(jnp.int32, (tk, tk), 0) >> 6
    qry_c = lax.broadcasted_iota(jnp.int32, (tk, tk), 1) >> 6
    diag_mask = key_c <= qry_c

    def k_tile(j):
        return k_ref[0, pl.ds(pl.multiple_of(j * tk, tk), tk), :]

    @pl.when(bounded_ref[0] != 0)
    def _():
        qt = qt_ref[0]
        acc_ref[...] = jnp.zeros_like(acc_ref)

        def full_tiles(jq, carry):
            tot = None
            for jj in range(per_q):
                j = jq * per_q + jj
                p = jnp.exp2(_dot(k_tile(j), qt)).astype(BF16)
                part = _dot(vt_ref[0, j], p)
                tot = part if tot is None else tot + part
            acc_ref[...] += tot
            return carry

        lax.fori_loop(0, i, full_tiles, 0)

        for jj in range(per_q):
            j = i * per_q + jj
            q0 = jj * tk
            p = jnp.exp2(_dot(k_tile(j), qt[:, q0:]))
            p_diag = jnp.where(diag_mask, p[:, :tk], 0.0)
            p = p_diag if q0 + tk == tq else jnp.concatenate([p_diag, p[:, tk:]], axis=1)
            acc_ref[:, q0:] += _dot(vt_ref[0, j], p.astype(BF16))

        acc = acc_ref[...]
        o_ref[...] = (acc[:MLA_V, :] / acc[MLA_V:MLA_V + 1, :]).astype(BF16)

    @pl.when(bounded_ref[0] == 0)
    def _():
        qt = qt_ref[0]

        def step(j, carry, mask):
            m, l, acc = carry
            s = _dot(k_tile(j), qt)
            if mask is not None:
                s = jnp.where(mask, s, -jnp.inf)
            m_new = jnp.maximum(m, jnp.max(s, axis=0, keepdims=True))
            alpha = jnp.exp2(m - m_new)
            p = jnp.exp2(s - m_new)
            l = alpha * l + jnp.sum(p, axis=0, keepdims=True)
            acc = alpha * acc + _dot(vt_ref[0, j][:MLA_V, :], p.astype(BF16))
            return m_new, l, acc

        carry = (jnp.full((1, tq), -jnp.inf, F32), jnp.zeros((1, tq), F32),
                 jnp.zeros((MLA_V, tq), F32))
        carry = lax.fori_loop(0, i * per_q, lambda j, c: step(j, c, None), carry)
        key_w = lax.broadcasted_iota(jnp.int32, (tk, tq), 0) >> 6
        qry_w = lax.broadcasted_iota(jnp.int32, (tk, tq), 1) >> 6
        for jj in range(per_q):
            carry = step(i * per_q + jj, carry, (key_w + jj * (tk // CHUNK)) <= qry_w)
        _, l, acc = carry
        o_ref[...] = (acc / l).astype(BF16)


def _mla_attention(bounded, qt, k, vt):
    heads, _, s = qt.shape
    tq, tk = ATTN_TQ, ATTN_TK
    kern = functools.partial(_attn_kernel, tq=tq, tk=tk)
    return pl.pallas_call(
        kern,
        out_shape=jax.ShapeDtypeStruct((heads * MLA_V, s), BF16),
        grid_spec=pltpu.PrefetchScalarGridSpec(
            num_scalar_prefetch=1,
            grid=(heads, s // tq),
            in_specs=[
                pl.BlockSpec((1, QK_PAD, tq), lambda h, i, b: (h, 0, i)),
                pl.BlockSpec((1, s, QK_PAD), lambda h, i, b: (h, 0, 0)),
                pl.BlockSpec((1, s // tk, V_AUG, tk), lambda h, i, b: (h, 0, 0, 0)),
            ],
            out_specs=pl.BlockSpec((MLA_V, tq), lambda h, i, b: (h, i)),
            scratch_shapes=[pltpu.VMEM((V_AUG, tq), F32)],
        ),
        compiler_params=_params(("parallel", "arbitrary")),
        name="mla_attn",
    )(bounded, qt, k, vt)


def kernel(x, positions, a_norm, a_w_in, a_w_gate_up, a_b_gate, a_out_norm, a_w_out, b_norm, b_w_dq,
           b_q_latent_norm, b_w_uq, b_q_norm, b_w_out, kv_norm, kv_w_down, kv_latent_norm, kv_w_up,
           k_norm, f_norm, f_w_in, f_w_out):
    batch, s, d = x.shape
    n_a = a_norm.shape[0]
    n_b = b_norm.shape[0]
    half = MLA_ROPE // 2
    inv_freq = (ROPE_THETA ** (-jnp.arange(half, dtype=F32) / half)).reshape(half, 1)
    outs = []
    for bi in range(batch):
        xb = x[bi]
        pos = positions[bi].reshape(1, s)
        k_sh = vt_sh = None
        for layer in range(n_a + n_b):
            if layer < n_a:
                xb = _gla_layer(xb, a_norm[layer], a_w_in[layer], a_w_gate_up[layer],
                                a_b_gate[layer], a_out_norm[layer], a_w_out[layer])
                xb = _ffn(xb, f_norm[layer], f_w_in[layer], f_w_out[layer])
            else:
                j = layer - n_a
                qt = _mla_q(xb, pos, inv_freq, b_norm[j], b_w_dq[j], b_q_latent_norm[j],
                            b_w_uq[j], b_q_norm[j])
                logit_bound = (LOGIT_BOUND_COEF * jnp.max(jnp.abs(b_q_norm[j]))
                               * jnp.max(jnp.abs(k_norm)))
                bounded = (logit_bound <= MAX_BOUNDED_LOGIT).astype(jnp.int32).reshape(1)
                attn_t = _mla_attention(bounded, qt, k_sh, vt_sh)
                xb = _ffn(xb, f_norm[layer], f_w_in[layer], f_w_out[layer],
                          attn_t=attn_t, w_o=b_w_out[j])
            if layer == n_a - 1 and n_b > 0:
                k_sh, vt_sh = _mla_kv(xb, pos, inv_freq, kv_norm, kv_w_down, kv_latent_norm,
                                      kv_w_up, k_norm)
        outs.append(xb)
    return jnp.stack(outs, axis=0)
```

```python
import functools

import jax
import jax.numpy as jnp
from jax import lax
from jax.experimental import pallas as pl
from jax.experimental.pallas import tpu as pltpu

F32 = jnp.float32
BF16 = jnp.bfloat16

EPS = 1e-6
CHUNK = 64
GLA_HEADS = 4
GLA_TAU = 16.0
GLA_GATE_RANK = 16
MLA_HEADS = 16
MLA_NOPE = 64
MLA_ROPE = 32
MLA_QK = MLA_NOPE + MLA_ROPE
MLA_V = 64
MLA_KV_RANK = 256
ROPE_THETA = 10000.0
LOG2_E = 1.4426950408889634

LANES = 128
BF16_SUBLANES = 16
QK_PAD = LANES
V_AUG = MLA_V + BF16_SUBLANES
LOGIT_BOUND_COEF = 1.02 * LOG2_E * MLA_QK ** 0.5
MAX_BOUNDED_LOGIT = 100.0
V7X_VMEM_LIMIT_BYTES = 56 * 1024 * 1024

GLA_ROWS = 256
FFN_ROWS = 256
ATTN_TQ = 1024
ATTN_TK = ATTN_TQ // 2
PROJ_ROWS = ATTN_TK


def _dot(a, b):
    return jnp.dot(a, b, preferred_element_type=F32)


def _dot_nt(a, b):
    return lax.dot_general(a, b, (((1,), (1,)), ((), ())), preferred_element_type=F32)


def _dot_tn(a, b):
    return lax.dot_general(a, b, (((0,), (0,)), ((), ())), preferred_element_type=F32)


def _rms_rows(x, g):
    return x * lax.rsqrt(jnp.mean(x * x, axis=-1, keepdims=True) + EPS) * g


def _silu(x):
    return x * jax.nn.sigmoid(x)


def _params(semantics):
    return pltpu.CompilerParams(dimension_semantics=semantics,
                                vmem_limit_bytes=V7X_VMEM_LIMIT_BYTES)


def _full(shape):
    return pl.BlockSpec(shape, lambda *_: (0,) * len(shape))


def _gla_kernel(x_ref, norm_ref, w_main_ref, w_gl_ref, w_gate_ref, b_gate_ref, onorm_ref,
                w_out_ref, out_ref, state_ref, proj_ref, b_ref, o_ref, *, rows, dk, dv):
    hk, hv = dk // GLA_HEADS, dv // GLA_HEADS

    @pl.when(pl.program_id(0) == 0)
    def _():
        state_ref[...] = jnp.zeros_like(state_ref)

    x = x_ref[...]
    h = _rms_rows(x, norm_ref[...]).astype(BF16)
    proj_ref[...] = _dot(h, w_main_ref[...])
    gl = _dot(h, w_gl_ref[...]).astype(BF16)
    z = _dot(gl, w_gate_ref[...]) + b_gate_ref[...]
    log_a = (jnp.minimum(z, 0.0) - jnp.log1p(jnp.exp(-jnp.abs(z)))) * (1.0 / GLA_TAU)

    r_i = lax.broadcasted_iota(jnp.int32, (rows, rows), 0)
    c_i = lax.broadcasted_iota(jnp.int32, (rows, rows), 1)
    tri = ((c_i <= r_i) & ((r_i >> 6) == (c_i >> 6))).astype(F32)
    b_ref[...] = jnp.dot(tri, log_a, preferred_element_type=F32, precision=lax.Precision.HIGHEST)

    qi = lax.broadcasted_iota(jnp.int32, (CHUNK, CHUNK), 0)
    kj = lax.broadcasted_iota(jnp.int32, (CHUNK, CHUNK), 1)
    causal = kj <= qi
    q_scale = hk ** -0.5

    for c in range(rows // CHUNK):
        r0 = c * CHUNK
        b_c = b_ref[r0:r0 + CHUNK, :]
        b_last = b_c[CHUNK - 1:CHUNK, :]
        q_c = proj_ref[r0:r0 + CHUNK, 0:dk]
        k_c = proj_ref[r0:r0 + CHUNK, dk:2 * dk]
        v_c = proj_ref[r0:r0 + CHUNK, 2 * dk:2 * dk + dv].astype(BF16)
        q_dec = (q_c * q_scale * jnp.exp(b_c)).astype(BF16)
        k_inv = (k_c * jnp.exp(-b_c)).astype(BF16)
        k_end = (k_c * jnp.exp(b_last - b_c)).astype(BF16)
        decay = jnp.exp(b_last)
        for hh in range(GLA_HEADS):
            ks = slice(hh * hk, (hh + 1) * hk)
            vs = slice(hh * hv, (hh + 1) * hv)
            attn = jnp.where(causal, _dot_nt(q_dec[:, ks], k_inv[:, ks]), 0.0).astype(BF16)
            state_t = state_ref[hh]
            o_h = _dot(attn, v_c[:, vs]) + _dot_nt(q_dec[:, ks], state_t.astype(BF16))
            state_ref[hh] = decay[:, ks] * state_t + _dot_tn(v_c[:, vs], k_end[:, ks])
            o_ref[r0:r0 + CHUNK, vs] = _rms_rows(o_h, onorm_ref[...])

    gate = _silu(proj_ref[:, 2 * dk + dv:2 * dk + 2 * dv])
    out_ref[...] = x + _dot((o_ref[...] * gate).astype(BF16), w_out_ref[...])


def _gla_layer(x, norm, w_in, w_gate_up, b_gate, out_norm, w_out):
    s, d = x.shape
    dk = w_gate_up.shape[1]
    dv = (w_in.shape[1] - 2 * dk - GLA_GATE_RANK) // 2
    n_main = 2 * dk + 2 * dv
    rows = GLA_ROWS
    w_main = w_in[:, :n_main].astype(BF16)
    w_gl = jnp.pad(w_in[:, n_main:], ((0, 0), (0, LANES - GLA_GATE_RANK))).astype(BF16)
    w_gate = jnp.pad(w_gate_up, ((0, LANES - GLA_GATE_RANK), (0, 0))).astype(BF16)
    kern = functools.partial(_gla_kernel, rows=rows, dk=dk, dv=dv)
    return pl.pallas_call(
        kern,
        out_shape=jax.ShapeDtypeStruct((s, d), F32),
        grid=(s // rows,),
        in_specs=[
            pl.BlockSpec((rows, d), lambda i: (i, 0)),
            _full((1, d)),
            _full((d, n_main)),
            _full((d, LANES)),
            _full((LANES, dk)),
            _full((1, dk)),
            _full((1, dv // GLA_HEADS)),
            _full((dv, d)),
        ],
        out_specs=pl.BlockSpec((rows, d), lambda i: (i, 0)),
        scratch_shapes=[
            pltpu.VMEM((GLA_HEADS, dv // GLA_HEADS, dk // GLA_HEADS), F32),
            pltpu.VMEM((rows, n_main), F32),
            pltpu.VMEM((rows, dk), F32),
            pltpu.VMEM((rows, dv), F32),
        ],
        compiler_params=_params(("arbitrary",)),
        name="gla_layer",
    )(x, norm.reshape(1, d), w_main, w_gl, w_gate, b_gate.reshape(1, dk),
      out_norm.reshape(1, -1), w_out.astype(BF16))


def _ffn_body(x, norm_ref, w_in_ref, w_out_ref, out_ref, hidden):
    h = _rms_rows(x, norm_ref[...]).astype(BF16)
    gate = _dot(h, w_in_ref[:, 0:hidden])
    up = _dot(h, w_in_ref[:, hidden:2 * hidden])
    act = (_silu(gate) * up).astype(BF16)
    out_ref[...] = x + _dot(act, w_out_ref[...])


def _ffn_kernel(x_ref, norm_ref, w_in_ref, w_out_ref, out_ref, *, hidden):
    _ffn_body(x_ref[...], norm_ref, w_in_ref, w_out_ref, out_ref, hidden)


def _proj_ffn_kernel(x_ref, ot_ref, w_o_ref, norm_ref, w_in_ref, w_out_ref, out_ref, *, hidden):
    x = x_ref[...] + _dot_tn(ot_ref[...], w_o_ref[...])
    _ffn_body(x, norm_ref, w_in_ref, w_out_ref, out_ref, hidden)


def _ffn(x, norm, w_in, w_out, attn_t=None, w_o=None):
    s, d = x.shape
    hidden = w_out.shape[0]
    rows = FFN_ROWS
    row_spec = pl.BlockSpec((rows, d), lambda i: (i, 0))
    ffn_specs = [_full((1, d)), _full((d, 2 * hidden)), _full((hidden, d))]
    ffn_args = (norm.reshape(1, d), w_in.astype(BF16), w_out.astype(BF16))
    if attn_t is None:
        kern = functools.partial(_ffn_kernel, hidden=hidden)
        in_specs = [row_spec] + ffn_specs
        args = (x,) + ffn_args
        name = "ffn"
    else:
        kern = functools.partial(_proj_ffn_kernel, hidden=hidden)
        in_specs = [row_spec, pl.BlockSpec((attn_t.shape[0], rows), lambda i: (0, i)),
                    _full(w_o.shape)] + ffn_specs
        args = (x, attn_t, w_o.astype(BF16)) + ffn_args
        name = "attn_proj_ffn"
    return pl.pallas_call(
        kern,
        out_shape=jax.ShapeDtypeStruct((s, d), F32),
        grid=(s // rows,),
        in_specs=in_specs,
        out_specs=row_spec,
        compiler_params=_params(("parallel",)),
        name=name,
    )(*args)


def _rope_tables(pos_ref, inv_freq_ref):
    ang = pos_ref[...].astype(F32) * inv_freq_ref[...]
    return jnp.cos(ang), jnp.sin(ang)


def _kv_kernel(x_ref, norm_ref, w_dc_ref, w_drt_ref, lnorm_ref, w_upt_ref, knorm_ref, pos_ref,
               inv_freq_ref, k_ref, vt_ref, kvt_ref, *, rows):
    half = MLA_ROPE // 2
    h = _rms_rows(x_ref[...], norm_ref[...]).astype(BF16)
    c_kv = _rms_rows(_dot(h, w_dc_ref[...]), lnorm_ref[...]).astype(BF16)
    kvt_ref[...] = _dot_nt(w_upt_ref[...], c_kv)
    kr = _dot_nt(w_drt_ref[...], h)
    cos, sin = _rope_tables(pos_ref, inv_freq_ref)
    g = knorm_ref[...]
    ss_rope = jnp.sum(kr * kr, axis=0, keepdims=True)
    krg = kr * g[MLA_NOPE:, :]
    x1, x2 = krg[:half, :], krg[half:, :]
    r1 = x1 * cos - x2 * sin
    r2 = x1 * sin + x2 * cos
    pad = jnp.zeros((QK_PAD - MLA_QK, rows), F32)
    ones_row = (lax.broadcasted_iota(jnp.int32, (V_AUG - MLA_V, rows), 0) == 0).astype(F32)
    per_head = MLA_NOPE + MLA_V
    for hh in range(MLA_HEADS):
        kn = kvt_ref[hh * per_head:hh * per_head + MLA_NOPE, :]
        ss = jnp.sum(kn * kn, axis=0, keepdims=True) + ss_rope
        inv = lax.rsqrt(ss * (1.0 / MLA_QK) + EPS)
        kt = jnp.concatenate([kn * inv * g[:MLA_NOPE, :], r1 * inv, r2 * inv, pad], axis=0)
        k_ref[hh] = kt.T.astype(BF16)
        vt = kvt_ref[hh * per_head + MLA_NOPE:(hh + 1) * per_head, :]
        vt_ref[hh, 0] = jnp.concatenate([vt, ones_row], axis=0).astype(BF16)


def _mla_kv(x, pos, inv_freq, kv_norm, w_down, latent_norm, w_up, k_norm):
    s, d = x.shape
    rows = PROJ_ROWS
    w_dc = w_down[:, :MLA_KV_RANK].astype(BF16)
    w_drt = w_down[:, MLA_KV_RANK:].T.astype(BF16)
    w_upt = w_up.T.astype(BF16)
    n_up = w_upt.shape[0]
    kern = functools.partial(_kv_kernel, rows=rows)
    return pl.pallas_call(
        kern,
        out_shape=(jax.ShapeDtypeStruct((MLA_HEADS, s, QK_PAD), BF16),
                   jax.ShapeDtypeStruct((MLA_HEADS, s // rows, V_AUG, rows), BF16)),
        grid=(s // rows,),
        in_specs=[
            pl.BlockSpec((rows, d), lambda i: (i, 0)),
            _full((1, d)),
            _full((d, MLA_KV_RANK)),
            _full((MLA_ROPE, d)),
            _full((1, MLA_KV_RANK)),
            _full((n_up, MLA_KV_RANK)),
            _full((MLA_QK, 1)),
            pl.BlockSpec((1, rows), lambda i: (0, i)),
            _full((MLA_ROPE // 2, 1)),
        ],
        out_specs=(pl.BlockSpec((MLA_HEADS, rows, QK_PAD), lambda i: (0, i, 0)),
                   pl.BlockSpec((MLA_HEADS, 1, V_AUG, rows), lambda i: (0, i, 0, 0))),
        scratch_shapes=[pltpu.VMEM((n_up, rows), F32)],
        compiler_params=_params(("parallel",)),
        name="mla_kv",
    )(x, kv_norm.reshape(1, d), w_dc, w_drt, latent_norm.reshape(1, -1), w_upt,
      k_norm.reshape(-1, 1), pos, inv_freq)


def _q_kernel(x_ref, norm_ref, w_dq_ref, qlnorm_ref, w_uqt_ref, qnorm_ref, pos_ref, inv_freq_ref,
              qt_ref, qall_ref, *, rows):
    half = MLA_ROPE // 2
    h = _rms_rows(x_ref[...], norm_ref[...]).astype(BF16)
    c_q = _rms_rows(_dot(h, w_dq_ref[...]), qlnorm_ref[...]).astype(BF16)
    qall_ref[...] = _dot_nt(w_uqt_ref[...], c_q)
    cos, sin = _rope_tables(pos_ref, inv_freq_ref)
    g = qnorm_ref[...]
    scale = MLA_QK ** -0.5 * LOG2_E
    pad = jnp.zeros((QK_PAD - MLA_QK, rows), F32)
    for hh in range(MLA_HEADS):
        q = qall_ref[hh * MLA_QK:(hh + 1) * MLA_QK, :]
        inv = lax.rsqrt(jnp.mean(q * q, axis=0, keepdims=True) + EPS)
        qn = q * inv * g
        x1, x2 = qn[MLA_NOPE:MLA_NOPE + half, :], qn[MLA_NOPE + half:, :]
        qt = jnp.concatenate([qn[:MLA_NOPE, :], x1 * cos - x2 * sin, x1 * sin + x2 * cos, pad],
                             axis=0)
        qt_ref[hh] = (qt * scale).astype(BF16)


def _mla_q(x, pos, inv_freq, norm, w_dq, q_latent_norm, w_uq, q_norm):
    s, d = x.shape
    rows = PROJ_ROWS
    rank = w_dq.shape[1]
    w_uqt = w_uq.T.astype(BF16)
    kern = functools.partial(_q_kernel, rows=rows)
    return pl.pallas_call(
        kern,
        out_shape=jax.ShapeDtypeStruct((MLA_HEADS, QK_PAD, s), BF16),
        grid=(s // rows,),
        in_specs=[
            pl.BlockSpec((rows, d), lambda i: (i, 0)),
            _full((1, d)),
            _full((d, rank)),
            _full((1, rank)),
            _full(w_uqt.shape),
            _full((MLA_QK, 1)),
            pl.BlockSpec((1, rows), lambda i: (0, i)),
            _full((MLA_ROPE // 2, 1)),
        ],
        out_specs=pl.BlockSpec((MLA_HEADS, QK_PAD, rows), lambda i: (0, 0, i)),
        scratch_shapes=[pltpu.VMEM(w_uqt.shape[:1] + (rows,), F32)],
        compiler_params=_params(("parallel",)),
        name="mla_q",
    )(x, norm.reshape(1, d), w_dq.astype(BF16), q_latent_norm.reshape(1, -1), w_uqt,
      q_norm.reshape(-1, 1), pos, inv_freq)


def _attn_kernel(bounded_ref, qt_ref, k_ref, vt_ref, o_ref, acc_ref, p0_ref, p1_ref, *, tq, tk):
    i = pl.program_id(1)
    per_q = tq // tk
    assert per_q == 2
    key_c = lax.broadcasted_iota(jnp.int32, (tk, tk), 0) >> 6
    qry_c = lax.broadcasted_iota(jnp.int32, (tk, tk), 1) >> 6
    diag_mask = key_c <= qry_c

    def k_tile(j):
        return k_ref[0, pl.ds(pl.multiple_of(j * tk, tk), tk), :]

    @pl.when(bounded_ref[0] != 0)
    def _():
        qt = qt_ref[0]

        def probs(j):
            return jnp.exp2(_dot(k_tile(j), qt)).astype(BF16)

        def weighted_values(p_ref, j):
            return _dot(vt_ref[0, j], p_ref[...])

        j0, j1 = 2 * i, 2 * i + 1
        p = jnp.exp2(_dot(k_tile(j0), qt))
        p0_ref[...] = jnp.concatenate([jnp.where(diag_mask, p[:, :tk], 0.0), p[:, tk:]],
                                      axis=1).astype(BF16)
        p = jnp.exp2(_dot(k_tile(j1), qt[:, tk:]))
        p1_ref[...] = jnp.concatenate([jnp.zeros((tk, tk), F32), jnp.where(diag_mask, p, 0.0)],
                                      axis=1).astype(BF16)
        acc_ref[...] = weighted_values(p0_ref, j0)

        def tile_pair(t, carry):
            p0_ref[...] = probs(2 * t)
            part = weighted_values(p1_ref, jnp.where(t == 0, j1, 2 * t - 1))
            p1_ref[...] = probs(2 * t + 1)
            acc_ref[...] += part + weighted_values(p0_ref, 2 * t)
            return carry

        lax.fori_loop(0, i, tile_pair, 0)
        acc = acc_ref[...] + weighted_values(p1_ref, jnp.where(i == 0, j1, 2 * i - 1))
        o_ref[...] = (acc[:MLA_V, :] / acc[MLA_V:MLA_V + 1, :]).astype(BF16)

    @pl.when(bounded_ref[0] == 0)
    def _():
        qt = qt_ref[0]

        def step(j, carry, mask):
            m, l, acc = carry
            s = _dot(k_tile(j), qt)
            if mask is not None:
                s = jnp.where(mask, s, -jnp.inf)
            m_new = jnp.maximum(m, jnp.max(s, axis=0, keepdims=True))
            alpha = jnp.exp2(m - m_new)
            p = jnp.exp2(s - m_new)
            l = alpha * l + jnp.sum(p, axis=0, keepdims=True)
            acc = alpha * acc + _dot(vt_ref[0, j][:MLA_V, :], p.astype(BF16))
            return m_new, l, acc

        carry = (jnp.full((1, tq), -jnp.inf, F32), jnp.zeros((1, tq), F32),
                 jnp.zeros((MLA_V, tq), F32))
        carry = lax.fori_loop(0, i * per_q, lambda j, c: step(j, c, None), carry)
        key_w = lax.broadcasted_iota(jnp.int32, (tk, tq), 0) >> 6
        qry_w = lax.broadcasted_iota(jnp.int32, (tk, tq), 1) >> 6
        for jj in range(per_q):
            carry = step(i * per_q + jj, carry, (key_w + jj * (tk // CHUNK)) <= qry_w)
        _, l, acc = carry
        o_ref[...] = (acc / l).astype(BF16)


def _mla_attention(bounded, qt, k, vt):
    heads, _, s = qt.shape
    tq, tk = ATTN_TQ, ATTN_TK
    kern = functools.partial(_attn_kernel, tq=tq, tk=tk)
    return pl.pallas_call(
        kern,
        out_shape=jax.ShapeDtypeStruct((heads * MLA_V, s), BF16),
        grid_spec=pltpu.PrefetchScalarGridSpec(
            num_scalar_prefetch=1,
            grid=(heads, s // tq),
            in_specs=[
                pl.BlockSpec((1, QK_PAD, tq), lambda h, i, b: (h, 0, i)),
                pl.BlockSpec((1, s, QK_PAD), lambda h, i, b: (h, 0, 0)),
                pl.BlockSpec((1, s // tk, V_AUG, tk), lambda h, i, b: (h, 0, 0, 0)),
            ],
            out_specs=pl.BlockSpec((MLA_V, tq), lambda h, i, b: (h, i)),
            scratch_shapes=[pltpu.VMEM((V_AUG, tq), F32), pltpu.VMEM((tk, tq), BF16),
                            pltpu.VMEM((tk, tq), BF16)],
        ),
        compiler_params=_params(("parallel", "arbitrary")),
        name="mla_attn",
    )(bounded, qt, k, vt)


def kernel(x, positions, a_norm, a_w_in, a_w_gate_up, a_b_gate, a_out_norm, a_w_out, b_norm, b_w_dq,
           b_q_latent_norm, b_w_uq, b_q_norm, b_w_out, kv_norm, kv_w_down, kv_latent_norm, kv_w_up,
           k_norm, f_norm, f_w_in, f_w_out):
    batch, s, d = x.shape
    n_a = a_norm.shape[0]
    n_b = b_norm.shape[0]
    half = MLA_ROPE // 2
    inv_freq = (ROPE_THETA ** (-jnp.arange(half, dtype=F32) / half)).reshape(half, 1)
    outs = []
    for bi in range(batch):
        xb = x[bi]
        pos = positions[bi].reshape(1, s)
        k_sh = vt_sh = None
        for layer in range(n_a + n_b):
            if layer < n_a:
                xb = _gla_layer(xb, a_norm[layer], a_w_in[layer], a_w_gate_up[layer],
                                a_b_gate[layer], a_out_norm[layer], a_w_out[layer])
                xb = _ffn(xb, f_norm[layer], f_w_in[layer], f_w_out[layer])
            else:
                j = layer - n_a
                qt = _mla_q(xb, pos, inv_freq, b_norm[j], b_w_dq[j], b_q_latent_norm[j],
                            b_w_uq[j], b_q_norm[j])
                logit_bound = (LOGIT_BOUND_COEF * jnp.max(jnp.abs(b_q_norm[j]))
                               * jnp.max(jnp.abs(k_norm)))
                bounded = (logit_bound <= MAX_BOUNDED_LOGIT).astype(jnp.int32).reshape(1)
                attn_t = _mla_attention(bounded, qt, k_sh, vt_sh)
                xb = _ffn(xb, f_norm[layer], f_w_in[layer], f_w_out[layer],
                          attn_t=attn_t, w_o=b_w_out[j])
            if layer == n_a - 1 and n_b > 0:
                k_sh, vt_sh = _mla_kv(xb, pos, inv_freq, kv_norm, kv_w_down, kv_latent_norm,
                                      kv_w_up, k_norm)
        outs.append(xb)
    return jnp.stack(outs, axis=0)
```

```python
import functools

import jax
import jax.numpy as jnp
from jax import lax
from jax.experimental import pallas as pl
from jax.experimental.pallas import tpu as pltpu

F32 = jnp.float32
BF16 = jnp.bfloat16

EPS = 1e-6
CHUNK = 64
GLA_HEADS = 4
GLA_TAU = 16.0
GLA_GATE_RANK = 16
MLA_HEADS = 16
MLA_NOPE = 64
MLA_ROPE = 32
MLA_QK = MLA_NOPE + MLA_ROPE
MLA_V = 64
MLA_KV_RANK = 256
ROPE_THETA = 10000.0
LOG2_E = 1.4426950408889634

LANES = 128
BF16_SUBLANES = 16
QK_PAD = LANES
V_AUG = MLA_V + BF16_SUBLANES
LOGIT_BOUND_COEF = 1.02 * LOG2_E * MLA_QK ** 0.5
MAX_BOUNDED_LOGIT = 100.0
V7X_VMEM_LIMIT_BYTES = 56 * 1024 * 1024

GLA_ROWS = 256
FFN_ROWS = 256
ATTN_TQ = 1024
ATTN_TK = ATTN_TQ // 2
PROJ_ROWS = ATTN_TK


def _dot(a, b):
    return jnp.dot(a, b, preferred_element_type=F32)


def _dot_nt(a, b):
    return lax.dot_general(a, b, (((1,), (1,)), ((), ())), preferred_element_type=F32)


def _dot_tn(a, b):
    return lax.dot_general(a, b, (((0,), (0,)), ((), ())), preferred_element_type=F32)


def _rms_rows(x, g):
    return x * lax.rsqrt(jnp.mean(x * x, axis=-1, keepdims=True) + EPS) * g


def _silu(x):
    return x * jax.nn.sigmoid(x)


def _params(semantics):
    return pltpu.CompilerParams(dimension_semantics=semantics,
                                vmem_limit_bytes=V7X_VMEM_LIMIT_BYTES)


def _full(shape):
    return pl.BlockSpec(shape, lambda *_: (0,) * len(shape))


def _gla_kernel(x_ref, norm_ref, w_main_ref, w_gl_ref, w_gate_ref, b_gate_ref, onorm_ref,
                w_out_ref, out_ref, state_ref, proj_ref, b_ref, o_ref, *, rows, dk, dv):
    hk, hv = dk // GLA_HEADS, dv // GLA_HEADS

    @pl.when(pl.program_id(0) == 0)
    def _():
        state_ref[...] = jnp.zeros_like(state_ref)

    x = x_ref[...]
    h = _rms_rows(x, norm_ref[...]).astype(BF16)
    proj_ref[...] = _dot(h, w_main_ref[...])
    gl = _dot(h, w_gl_ref[...]).astype(BF16)
    z = _dot(gl, w_gate_ref[...]) + b_gate_ref[...]
    log_a = (jnp.minimum(z, 0.0) - jnp.log1p(jnp.exp(-jnp.abs(z)))) * (1.0 / GLA_TAU)

    r_i = lax.broadcasted_iota(jnp.int32, (rows, rows), 0)
    c_i = lax.broadcasted_iota(jnp.int32, (rows, rows), 1)
    tri = ((c_i <= r_i) & ((r_i >> 6) == (c_i >> 6))).astype(BF16)
    hi = log_a.astype(BF16)
    rest = log_a - hi.astype(F32)
    mid = rest.astype(BF16)
    lo = (rest - mid.astype(F32)).astype(BF16)
    b_ref[...] = _dot(tri, hi) + _dot(tri, mid) + _dot(tri, lo)

    qi = lax.broadcasted_iota(jnp.int32, (CHUNK, CHUNK), 0)
    kj = lax.broadcasted_iota(jnp.int32, (CHUNK, CHUNK), 1)
    causal = kj <= qi
    q_scale = hk ** -0.5

    for c in range(rows // CHUNK):
        r0 = c * CHUNK
        b_c = b_ref[r0:r0 + CHUNK, :]
        b_last = b_c[CHUNK - 1:CHUNK, :]
        q_c = proj_ref[r0:r0 + CHUNK, 0:dk]
        k_c = proj_ref[r0:r0 + CHUNK, dk:2 * dk]
        v_c = proj_ref[r0:r0 + CHUNK, 2 * dk:2 * dk + dv].astype(BF16)
        q_dec = (q_c * q_scale * jnp.exp(b_c)).astype(BF16)
        k_inv = (k_c * jnp.exp(-b_c)).astype(BF16)
        k_end = (k_c * jnp.exp(b_last - b_c)).astype(BF16)
        decay = jnp.exp(b_last)
        for hh in range(GLA_HEADS):
            ks = slice(hh * hk, (hh + 1) * hk)
            vs = slice(hh * hv, (hh + 1) * hv)
            attn = jnp.where(causal, _dot_nt(q_dec[:, ks], k_inv[:, ks]), 0.0).astype(BF16)
            state_t = state_ref[hh]
            o_h = _dot(attn, v_c[:, vs]) + _dot_nt(q_dec[:, ks], state_t.astype(BF16))
            state_ref[hh] = decay[:, ks] * state_t + _dot_tn(v_c[:, vs], k_end[:, ks])
            o_ref[r0:r0 + CHUNK, vs] = _rms_rows(o_h, onorm_ref[...])

    gate = _silu(proj_ref[:, 2 * dk + dv:2 * dk + 2 * dv])
    out_ref[...] = x + _dot((o_ref[...] * gate).astype(BF16), w_out_ref[...])


def _gla_layer(x, norm, w_in, w_gate_up, b_gate, out_norm, w_out):
    s, d = x.shape
    dk = w_gate_up.shape[1]
    dv = (w_in.shape[1] - 2 * dk - GLA_GATE_RANK) // 2
    n_main = 2 * dk + 2 * dv
    rows = GLA_ROWS
    w_main = w_in[:, :n_main].astype(BF16)
    w_gl = jnp.pad(w_in[:, n_main:], ((0, 0), (0, LANES - GLA_GATE_RANK))).astype(BF16)
    w_gate = jnp.pad(w_gate_up, ((0, LANES - GLA_GATE_RANK), (0, 0))).astype(BF16)
    kern = functools.partial(_gla_kernel, rows=rows, dk=dk, dv=dv)
    return pl.pallas_call(
        kern,
        out_shape=jax.ShapeDtypeStruct((s, d), F32),
        grid=(s // rows,),
        in_specs=[
            pl.BlockSpec((rows, d), lambda i: (i, 0)),
            _full((1, d)),
            _full((d, n_main)),
            _full((d, LANES)),
            _full((LANES, dk)),
            _full((1, dk)),
            _full((1, dv // GLA_HEADS)),
            _full((dv, d)),
        ],
        out_specs=pl.BlockSpec((rows, d), lambda i: (i, 0)),
        scratch_shapes=[
            pltpu.VMEM((GLA_HEADS, dv // GLA_HEADS, dk // GLA_HEADS), F32),
            pltpu.VMEM((rows, n_main), F32),
            pltpu.VMEM((rows, dk), F32),
            pltpu.VMEM((rows, dv), F32),
        ],
        compiler_params=_params(("arbitrary",)),
        name="gla_layer",
    )(x, norm.reshape(1, d), w_main, w_gl, w_gate, b_gate.reshape(1, dk),
      out_norm.reshape(1, -1), w_out.astype(BF16))


def _ffn_body(x, norm_ref, w_in_ref, w_out_ref, out_ref, hidden):
    h = _rms_rows(x, norm_ref[...]).astype(BF16)
    gate = _dot(h, w_in_ref[:, 0:hidden])
    up = _dot(h, w_in_ref[:, hidden:2 * hidden])
    act = (_silu(gate) * up).astype(BF16)
    out_ref[...] = x + _dot(act, w_out_ref[...])


def _ffn_kernel(x_ref, norm_ref, w_in_ref, w_out_ref, out_ref, *, hidden):
    _ffn_body(x_ref[...], norm_ref, w_in_ref, w_out_ref, out_ref, hidden)


def _proj_ffn_kernel(x_ref, ot_ref, w_o_ref, norm_ref, w_in_ref, w_out_ref, out_ref, *, hidden):
    x = x_ref[...] + _dot_tn(ot_ref[...], w_o_ref[...])
    _ffn_body(x, norm_ref, w_in_ref, w_out_ref, out_ref, hidden)


def _ffn(x, norm, w_in, w_out, attn_t=None, w_o=None):
    s, d = x.shape
    hidden = w_out.shape[0]
    rows = FFN_ROWS
    row_spec = pl.BlockSpec((rows, d), lambda i: (i, 0))
    ffn_specs = [_full((1, d)), _full((d, 2 * hidden)), _full((hidden, d))]
    ffn_args = (norm.reshape(1, d), w_in.astype(BF16), w_out.astype(BF16))
    if attn_t is None:
        kern = functools.partial(_ffn_kernel, hidden=hidden)
        in_specs = [row_spec] + ffn_specs
        args = (x,) + ffn_args
        name = "ffn"
    else:
        kern = functools.partial(_proj_ffn_kernel, hidden=hidden)
        in_specs = [row_spec, pl.BlockSpec((attn_t.shape[0], rows), lambda i: (0, i)),
                    _full(w_o.shape)] + ffn_specs
        args = (x, attn_t, w_o.astype(BF16)) + ffn_args
        name = "attn_proj_ffn"
    return pl.pallas_call(
        kern,
        out_shape=jax.ShapeDtypeStruct((s, d), F32),
        grid=(s // rows,),
        in_specs=in_specs,
        out_specs=row_spec,
        compiler_params=_params(("parallel",)),
        name=name,
    )(*args)


def _rope_tables(pos_ref, inv_freq_ref):
    ang = pos_ref[...].astype(F32) * inv_freq_ref[...]
    return jnp.cos(ang), jnp.sin(ang)


def _kv_kernel(x_ref, norm_ref, w_dc_ref, w_drt_ref, lnorm_ref, w_upt_ref, knorm_ref, pos_ref,
               inv_freq_ref, k_ref, vt_ref, kvt_ref, *, rows):
    half = MLA_ROPE // 2
    h = _rms_rows(x_ref[...], norm_ref[...]).astype(BF16)
    c_kv = _rms_rows(_dot(h, w_dc_ref[...]), lnorm_ref[...]).astype(BF16)
    kvt_ref[...] = _dot_nt(w_upt_ref[...], c_kv)
    kr = _dot_nt(w_drt_ref[...], h)
    cos, sin = _rope_tables(pos_ref, inv_freq_ref)
    g = knorm_ref[...]
    ss_rope = jnp.sum(kr * kr, axis=0, keepdims=True)
    krg = kr * g[MLA_NOPE:, :]
    x1, x2 = krg[:half, :], krg[half:, :]
    r1 = x1 * cos - x2 * sin
    r2 = x1 * sin + x2 * cos
    pad = jnp.zeros((QK_PAD - MLA_QK, rows), F32)
    ones_row = (lax.broadcasted_iota(jnp.int32, (V_AUG - MLA_V, rows), 0) == 0).astype(F32)
    per_head = MLA_NOPE + MLA_V
    for hh in range(MLA_HEADS):
        kn = kvt_ref[hh * per_head:hh * per_head + MLA_NOPE, :]
        ss = jnp.sum(kn * kn, axis=0, keepdims=True) + ss_rope
        inv = lax.rsqrt(ss * (1.0 / MLA_QK) + EPS)
        kt = jnp.concatenate([kn * inv * g[:MLA_NOPE, :], r1 * inv, r2 * inv, pad], axis=0)
        k_ref[hh] = kt.T.astype(BF16)
        vt = kvt_ref[hh * per_head + MLA_NOPE:(hh + 1) * per_head, :]
        vt_ref[hh, 0] = jnp.concatenate([vt, ones_row], axis=0).astype(BF16)


def _mla_kv(x, pos, inv_freq, kv_norm, w_down, latent_norm, w_up, k_norm):
    s, d = x.shape
    rows = PROJ_ROWS
    w_dc = w_down[:, :MLA_KV_RANK].astype(BF16)
    w_drt = w_down[:, MLA_KV_RANK:].T.astype(BF16)
    w_upt = w_up.T.astype(BF16)
    n_up = w_upt.shape[0]
    kern = functools.partial(_kv_kernel, rows=rows)
    return pl.pallas_call(
        kern,
        out_shape=(jax.ShapeDtypeStruct((MLA_HEADS, s, QK_PAD), BF16),
                   jax.ShapeDtypeStruct((MLA_HEADS, s // rows, V_AUG, rows), BF16)),
        grid=(s // rows,),
        in_specs=[
            pl.BlockSpec((rows, d), lambda i: (i, 0)),
            _full((1, d)),
            _full((d, MLA_KV_RANK)),
            _full((MLA_ROPE, d)),
            _full((1, MLA_KV_RANK)),
            _full((n_up, MLA_KV_RANK)),
            _full((MLA_QK, 1)),
            pl.BlockSpec((1, rows), lambda i: (0, i)),
            _full((MLA_ROPE // 2, 1)),
        ],
        out_specs=(pl.BlockSpec((MLA_HEADS, rows, QK_PAD), lambda i: (0, i, 0)),
                   pl.BlockSpec((MLA_HEADS, 1, V_AUG, rows), lambda i: (0, i, 0, 0))),
        scratch_shapes=[pltpu.VMEM((n_up, rows), F32)],
        compiler_params=_params(("parallel",)),
        name="mla_kv",
    )(x, kv_norm.reshape(1, d), w_dc, w_drt, latent_norm.reshape(1, -1), w_upt,
      k_norm.reshape(-1, 1), pos, inv_freq)


def _q_kernel(x_ref, norm_ref, w_dq_ref, qlnorm_ref, w_uqt_ref, qnorm_ref, pos_ref, inv_freq_ref,
              qt_ref, qall_ref, *, rows):
    half = MLA_ROPE // 2
    h = _rms_rows(x_ref[...], norm_ref[...]).astype(BF16)
    c_q = _rms_rows(_dot(h, w_dq_ref[...]), qlnorm_ref[...]).astype(BF16)
    qall_ref[...] = _dot_nt(w_uqt_ref[...], c_q)
    cos, sin = _rope_tables(pos_ref, inv_freq_ref)
    g = qnorm_ref[...]
    scale = MLA_QK ** -0.5 * LOG2_E
    pad = jnp.zeros((QK_PAD - MLA_QK, rows), F32)
    for hh in range(MLA_HEADS):
        q = qall_ref[hh * MLA_QK:(hh + 1) * MLA_QK, :]
        inv = lax.rsqrt(jnp.mean(q * q, axis=0, keepdims=True) + EPS)
        qn = q * inv * g
        x1, x2 = qn[MLA_NOPE:MLA_NOPE + half, :], qn[MLA_NOPE + half:, :]
        qt = jnp.concatenate([qn[:MLA_NOPE, :], x1 * cos - x2 * sin, x1 * sin + x2 * cos, pad],
                             axis=0)
        qt_ref[hh] = (qt * scale).astype(BF16)


def _mla_q(x, pos, inv_freq, norm, w_dq, q_latent_norm, w_uq, q_norm):
    s, d = x.shape
    rows = PROJ_ROWS
    rank = w_dq.shape[1]
    w_uqt = w_uq.T.astype(BF16)
    kern = functools.partial(_q_kernel, rows=rows)
    return pl.pallas_call(
        kern,
        out_shape=jax.ShapeDtypeStruct((MLA_HEADS, QK_PAD, s), BF16),
        grid=(s // rows,),
        in_specs=[
            pl.BlockSpec((rows, d), lambda i: (i, 0)),
            _full((1, d)),
            _full((d, rank)),
            _full((1, rank)),
            _full(w_uqt.shape),
            _full((MLA_QK, 1)),
            pl.BlockSpec((1, rows), lambda i: (0, i)),
            _full((MLA_ROPE // 2, 1)),
        ],
        out_specs=pl.BlockSpec((MLA_HEADS, QK_PAD, rows), lambda i: (0, 0, i)),
        scratch_shapes=[pltpu.VMEM(w_uqt.shape[:1] + (rows,), F32)],
        compiler_params=_params(("parallel",)),
        name="mla_q",
    )(x, norm.reshape(1, d), w_dq.astype(BF16), q_latent_norm.reshape(1, -1), w_uqt,
      q_norm.reshape(-1, 1), pos, inv_freq)


def _attn_kernel(bounded_ref, qt_ref, k_ref, vt_ref, o_ref, acc_ref, p0_ref, p1_ref, *, tq, tk):
    i = pl.program_id(1)
    per_q = tq // tk
    assert per_q == 2
    key_c = lax.broadcasted_iota(jnp.int32, (tk, tk), 0) >> 6
    qry_c = lax.broadcasted_iota(jnp.int32, (tk, tk), 1) >> 6
    diag_mask = key_c <= qry_c

    def k_tile(j):
        return k_ref[0, pl.ds(pl.multiple_of(j * tk, tk), tk), :]

    @pl.when(bounded_ref[0] != 0)
    def _():
        qt = qt_ref[0]

        def probs(j):
            return jnp.exp2(_dot(k_tile(j), qt)).astype(BF16)

        def weighted_values(p_ref, j):
            return _dot(vt_ref[0, j], p_ref[...])

        j0, j1 = 2 * i, 2 * i + 1
        p = jnp.exp2(_dot(k_tile(j0), qt))
        p0_ref[...] = jnp.concatenate([jnp.where(diag_mask, p[:, :tk], 0.0), p[:, tk:]],
                                      axis=1).astype(BF16)
        p = jnp.exp2(_dot(k_tile(j1), qt[:, tk:]))
        p1_ref[...] = jnp.concatenate([jnp.zeros((tk, tk), F32), jnp.where(diag_mask, p, 0.0)],
                                      axis=1).astype(BF16)
        acc_ref[...] = weighted_values(p0_ref, j0)

        def tile_pairs(t, n_pairs):
            tot = None
            for u in range(n_pairs):
                p0_ref[...] = probs(2 * (t + u))
                prev = 2 * (t + u) - 1
                part = weighted_values(p1_ref, jnp.where(t == 0, j1, prev) if u == 0 else prev)
                p1_ref[...] = probs(2 * (t + u) + 1)
                part = part + weighted_values(p0_ref, 2 * (t + u))
                tot = part if tot is None else tot + part
            acc_ref[...] += tot

        def two_pairs(u, carry):
            tile_pairs(2 * u, 2)
            return carry

        def one_pair(u, carry):
            tile_pairs(2 * (i >> 1) + u, 1)
            return carry

        lax.fori_loop(0, i >> 1, two_pairs, 0)
        lax.fori_loop(0, i & 1, one_pair, 0)
        acc = acc_ref[...] + weighted_values(p1_ref, jnp.where(i == 0, j1, 2 * i - 1))
        o_ref[...] = (acc[:MLA_V, :] / acc[MLA_V:MLA_V + 1, :]).astype(BF16)

    @pl.when(bounded_ref[0] == 0)
    def _():
        qt = qt_ref[0]

        def step(j, carry, mask):
            m, l, acc = carry
            s = _dot(k_tile(j), qt)
            if mask is not None:
                s = jnp.where(mask, s, -jnp.inf)
            m_new = jnp.maximum(m, jnp.max(s, axis=0, keepdims=True))
            alpha = jnp.exp2(m - m_new)
            p = jnp.exp2(s - m_new)
            l = alpha * l + jnp.sum(p, axis=0, keepdims=True)
            acc = alpha * acc + _dot(vt_ref[0, j][:MLA_V, :], p.astype(BF16))
            return m_new, l, acc

        carry = (jnp.full((1, tq), -jnp.inf, F32), jnp.zeros((1, tq), F32),
                 jnp.zeros((MLA_V, tq), F32))
        carry = lax.fori_loop(0, i * per_q, lambda j, c: step(j, c, None), carry)
        key_w = lax.broadcasted_iota(jnp.int32, (tk, tq), 0) >> 6
        qry_w = lax.broadcasted_iota(jnp.int32, (tk, tq), 1) >> 6
        for jj in range(per_q):
            carry = step(i * per_q + jj, carry, (key_w + jj * (tk // CHUNK)) <= qry_w)
        _, l, acc = carry
        o_ref[...] = (acc / l).astype(BF16)


def _mla_attention(bounded, qt, k, vt):
    heads, _, s = qt.shape
    tq, tk = ATTN_TQ, ATTN_TK
    kern = functools.partial(_attn_kernel, tq=tq, tk=tk)
    return pl.pallas_call(
        kern,
        out_shape=jax.ShapeDtypeStruct((heads * MLA_V, s), BF16),
        grid_spec=pltpu.PrefetchScalarGridSpec(
            num_scalar_prefetch=1,
            grid=(heads, s // tq),
            in_specs=[
                pl.BlockSpec((1, QK_PAD, tq), lambda h, i, b: (h, 0, i)),
                pl.BlockSpec((1, s, QK_PAD), lambda h, i, b: (h, 0, 0)),
                pl.BlockSpec((1, s // tk, V_AUG, tk), lambda h, i, b: (h, 0, 0, 0)),
            ],
            out_specs=pl.BlockSpec((MLA_V, tq), lambda h, i, b: (h, i)),
            scratch_shapes=[pltpu.VMEM((V_AUG, tq), F32), pltpu.VMEM((tk, tq), BF16),
                            pltpu.VMEM((tk, tq), BF16)],
        ),
        compiler_params=_params(("parallel", "arbitrary")),
        name="mla_attn",
    )(bounded, qt, k, vt)


def kernel(x, positions, a_norm, a_w_in, a_w_gate_up, a_b_gate, a_out_norm, a_w_out, b_norm, b_w_dq,
           b_q_latent_norm, b_w_uq, b_q_norm, b_w_out, kv_norm, kv_w_down, kv_latent_norm, kv_w_up,
           k_norm, f_norm, f_w_in, f_w_out):
    batch, s, d = x.shape
    n_a = a_norm.shape[0]
    n_b = b_norm.shape[0]
    half = MLA_ROPE // 2
    inv_freq = (ROPE_THETA ** (-jnp.arange(half, dtype=F32) / half)).reshape(half, 1)
    outs = []
    for bi in range(batch):
        xb = x[bi]
        pos = positions[bi].reshape(1, s)
        k_sh = vt_sh = None
        for layer in range(n_a + n_b):
            if layer < n_a:
                xb = _gla_layer(xb, a_norm[layer], a_w_in[layer], a_w_gate_up[layer],
                                a_b_gate[layer], a_out_norm[layer], a_w_out[layer])
                xb = _ffn(xb, f_norm[layer], f_w_in[layer], f_w_out[layer])
            else:
                j = layer - n_a
                qt = _mla_q(xb, pos, inv_freq, b_norm[j], b_w_dq[j], b_q_latent_norm[j],
                            b_w_uq[j], b_q_norm[j])
                logit_bound = (LOGIT_BOUND_COEF * jnp.max(jnp.abs(b_q_norm[j]))
                               * jnp.max(jnp.abs(k_norm)))
                bounded = (logit_bound <= MAX_BOUNDED_LOGIT).astype(jnp.int32).reshape(1)
                attn_t = _mla_attention(bounded, qt, k_sh, vt_sh)
                xb = _ffn(xb, f_norm[layer], f_w_in[layer], f_w_out[layer],
                          attn_t=attn_t, w_o=b_w_out[j])
            if layer == n_a - 1 and n_b > 0:
                k_sh, vt_sh = _mla_kv(xb, pos, inv_freq, kv_norm, kv_w_down, kv_latent_norm,
                                      kv_w_up, k_norm)
        outs.append(xb)
    return jnp.stack(outs, axis=0)
```

```python
import functools

import jax
import jax.numpy as jnp
from jax import lax
from jax.experimental import pallas as pl
from jax.experimental.pallas import tpu as pltpu

F32 = jnp.float32
BF16 = jnp.bfloat16

EPS = 1e-6
CHUNK = 64
GLA_HEADS = 4
GLA_TAU = 16.0
GLA_GATE_RANK = 16
MLA_HEADS = 16
MLA_NOPE = 64
MLA_ROPE = 32
MLA_QK = MLA_NOPE + MLA_ROPE
MLA_V = 64
MLA_KV_RANK = 256
ROPE_THETA = 10000.0
LOG2_E = 1.4426950408889634

LANES = 128
BF16_SUBLANES = 16
QK_PAD = LANES
V_AUG = MLA_V + BF16_SUBLANES
LOGIT_BOUND_COEF = 1.02 * LOG2_E * MLA_QK ** 0.5
MAX_BOUNDED_LOGIT = 100.0
V7X_VMEM_LIMIT_BYTES = 56 * 1024 * 1024

GLA_ROWS = 256
FFN_ROWS = 256
ATTN_TQ = 1024
ATTN_TK = ATTN_TQ // 2
PROJ_ROWS = ATTN_TK


def _dot(a, b):
    return jnp.dot(a, b, preferred_element_type=F32)


def _dot_nt(a, b):
    return lax.dot_general(a, b, (((1,), (1,)), ((), ())), preferred_element_type=F32)


def _dot_tn(a, b):
    return lax.dot_general(a, b, (((0,), (0,)), ((), ())), preferred_element_type=F32)


def _rms_rows(x, g):
    return x * lax.rsqrt(jnp.mean(x * x, axis=-1, keepdims=True) + EPS) * g


def _silu(x):
    return x * jax.nn.sigmoid(x)


def _params(semantics):
    return pltpu.CompilerParams(dimension_semantics=semantics,
                                vmem_limit_bytes=V7X_VMEM_LIMIT_BYTES)


def _full(shape):
    return pl.BlockSpec(shape, lambda *_: (0,) * len(shape))


def _gla_kernel(x_ref, norm_ref, w_main_ref, w_gl_ref, w_gate_ref, b_gate_ref, onorm_ref,
                w_out_ref, out_ref, state_ref, proj_ref, b_ref, o_ref, *, rows, dk, dv):
    hk, hv = dk // GLA_HEADS, dv // GLA_HEADS

    @pl.when(pl.program_id(0) == 0)
    def _():
        state_ref[...] = jnp.zeros_like(state_ref)

    x = x_ref[...]
    h = _rms_rows(x, norm_ref[...]).astype(BF16)
    proj_ref[...] = _dot(h, w_main_ref[...])
    gl = _dot(h, w_gl_ref[...]).astype(BF16)
    z = _dot(gl, w_gate_ref[...]) + b_gate_ref[...]
    log_a = (jnp.minimum(z, 0.0) - jnp.log1p(jnp.exp(-jnp.abs(z)))) * (1.0 / GLA_TAU)

    r_i = lax.broadcasted_iota(jnp.int32, (rows, rows), 0)
    c_i = lax.broadcasted_iota(jnp.int32, (rows, rows), 1)
    tri = ((c_i <= r_i) & ((r_i >> 6) == (c_i >> 6))).astype(BF16)
    hi = log_a.astype(BF16)
    rest = log_a - hi.astype(F32)
    mid = rest.astype(BF16)
    lo = (rest - mid.astype(F32)).astype(BF16)
    b_ref[...] = _dot(tri, hi) + _dot(tri, mid) + _dot(tri, lo)

    qi = lax.broadcasted_iota(jnp.int32, (CHUNK, CHUNK), 0)
    kj = lax.broadcasted_iota(jnp.int32, (CHUNK, CHUNK), 1)
    causal = kj <= qi
    q_scale = hk ** -0.5

    for c in range(rows // CHUNK):
        r0 = c * CHUNK
        b_c = b_ref[r0:r0 + CHUNK, :]
        b_last = b_c[CHUNK - 1:CHUNK, :]
        q_c = proj_ref[r0:r0 + CHUNK, 0:dk]
        k_c = proj_ref[r0:r0 + CHUNK, dk:2 * dk]
        v_c = proj_ref[r0:r0 + CHUNK, 2 * dk:2 * dk + dv].astype(BF16)
        q_dec = (q_c * q_scale * jnp.exp(b_c)).astype(BF16)
        k_inv = (k_c * jnp.exp(-b_c)).astype(BF16)
        k_end = (k_c * jnp.exp(b_last - b_c)).astype(BF16)
        decay = jnp.exp(b_last)
        for hh in range(GLA_HEADS):
            ks = slice(hh * hk, (hh + 1) * hk)
            vs = slice(hh * hv, (hh + 1) * hv)
            attn = jnp.where(causal, _dot_nt(q_dec[:, ks], k_inv[:, ks]), 0.0).astype(BF16)
            state_t = state_ref[hh]
            o_h = _dot(attn, v_c[:, vs]) + _dot_nt(q_dec[:, ks], state_t.astype(BF16))
            state_ref[hh] = decay[:, ks] * state_t + _dot_tn(v_c[:, vs], k_end[:, ks])
            o_ref[r0:r0 + CHUNK, vs] = _rms_rows(o_h, onorm_ref[...])

    gate = _silu(proj_ref[:, 2 * dk + dv:2 * dk + 2 * dv])
    out_ref[...] = x + _dot((o_ref[...] * gate).astype(BF16), w_out_ref[...])


def _gla_layer(x, norm, w_in, w_gate_up, b_gate, out_norm, w_out):
    s, d = x.shape
    dk = w_gate_up.shape[1]
    dv = (w_in.shape[1] - 2 * dk - GLA_GATE_RANK) // 2
    n_main = 2 * dk + 2 * dv
    rows = GLA_ROWS
    w_main = w_in[:, :n_main].astype(BF16)
    w_gl = jnp.pad(w_in[:, n_main:], ((0, 0), (0, LANES - GLA_GATE_RANK))).astype(BF16)
    w_gate = jnp.pad(w_gate_up, ((0, LANES - GLA_GATE_RANK), (0, 0))).astype(BF16)
    kern = functools.partial(_gla_kernel, rows=rows, dk=dk, dv=dv)
    return pl.pallas_call(
        kern,
        out_shape=jax.ShapeDtypeStruct((s, d), F32),
        grid=(s // rows,),
        in_specs=[
            pl.BlockSpec((rows, d), lambda i: (i, 0)),
            _full((1, d)),
            _full((d, n_main)),
            _full((d, LANES)),
            _full((LANES, dk)),
            _full((1, dk)),
            _full((1, dv // GLA_HEADS)),
            _full((dv, d)),
        ],
        out_specs=pl.BlockSpec((rows, d), lambda i: (i, 0)),
        scratch_shapes=[
            pltpu.VMEM((GLA_HEADS, dv // GLA_HEADS, dk // GLA_HEADS), F32),
            pltpu.VMEM((rows, n_main), F32),
            pltpu.VMEM((rows, dk), F32),
            pltpu.VMEM((rows, dv), F32),
        ],
        compiler_params=_params(("arbitrary",)),
        name="gla_layer",
    )(x, norm.reshape(1, d), w_main, w_gl, w_gate, b_gate.reshape(1, dk),
      out_norm.reshape(1, -1), w_out.astype(BF16))


def _ffn_body(x, norm_ref, w_in_ref, w_out_ref, out_ref, hidden):
    h = _rms_rows(x, norm_ref[...]).astype(BF16)
    gate = _dot(h, w_in_ref[:, 0:hidden])
    up = _dot(h, w_in_ref[:, hidden:2 * hidden])
    act = (_silu(gate) * up).astype(BF16)
    out_ref[...] = x + _dot(act, w_out_ref[...])


def _ffn_kernel(x_ref, norm_ref, w_in_ref, w_out_ref, out_ref, *, hidden):
    _ffn_body(x_ref[...], norm_ref, w_in_ref, w_out_ref, out_ref, hidden)


def _proj_ffn_kernel(x_ref, ot_ref, w_o_ref, norm_ref, w_in_ref, w_out_ref, out_ref, *, hidden):
    x = x_ref[...] + _dot_tn(ot_ref[...], w_o_ref[...])
    _ffn_body(x, norm_ref, w_in_ref, w_out_ref, out_ref, hidden)


def _ffn(x, norm, w_in, w_out, attn_t=None, w_o=None):
    s, d = x.shape
    hidden = w_out.shape[0]
    rows = FFN_ROWS
    row_spec = pl.BlockSpec((rows, d), lambda i: (i, 0))
    ffn_specs = [_full((1, d)), _full((d, 2 * hidden)), _full((hidden, d))]
    ffn_args = (norm.reshape(1, d), w_in.astype(BF16), w_out.astype(BF16))
    if attn_t is None:
        kern = functools.partial(_ffn_kernel, hidden=hidden)
        in_specs = [row_spec] + ffn_specs
        args = (x,) + ffn_args
        name = "ffn"
    else:
        kern = functools.partial(_proj_ffn_kernel, hidden=hidden)
        in_specs = [row_spec, pl.BlockSpec((attn_t.shape[0], rows), lambda i: (0, i)),
                    _full(w_o.shape)] + ffn_specs
        args = (x, attn_t, w_o.astype(BF16)) + ffn_args
        name = "attn_proj_ffn"
    return pl.pallas_call(
        kern,
        out_shape=jax.ShapeDtypeStruct((s, d), F32),
        grid=(s // rows,),
        in_specs=in_specs,
        out_specs=row_spec,
        compiler_params=_params(("parallel",)),
        name=name,
    )(*args)


def _rope_tables(pos_ref, inv_freq_ref):
    ang = pos_ref[...].astype(F32) * inv_freq_ref[...]
    return jnp.cos(ang), jnp.sin(ang)


def _kv_kernel(x_ref, norm_ref, w_dc_ref, w_drt_ref, lnorm_ref, w_upt_ref, knorm_ref, pos_ref,
               inv_freq_ref, k_ref, vt_ref, kvt_ref, *, rows):
    half = MLA_ROPE // 2
    h = _rms_rows(x_ref[...], norm_ref[...]).astype(BF16)
    c_kv = _rms_rows(_dot(h, w_dc_ref[...]), lnorm_ref[...]).astype(BF16)
    kvt_ref[...] = _dot_nt(w_upt_ref[...], c_kv)
    kr = _dot_nt(w_drt_ref[...], h)
    cos, sin = _rope_tables(pos_ref, inv_freq_ref)
    g = knorm_ref[...]
    ss_rope = jnp.sum(kr * kr, axis=0, keepdims=True)
    krg = kr * g[MLA_NOPE:, :]
    x1, x2 = krg[:half, :], krg[half:, :]
    r1 = x1 * cos - x2 * sin
    r2 = x1 * sin + x2 * cos
    pad = jnp.zeros((QK_PAD - MLA_QK, rows), F32)
    ones_row = (lax.broadcasted_iota(jnp.int32, (V_AUG - MLA_V, rows), 0) == 0).astype(F32)
    per_head = MLA_NOPE + MLA_V
    for hh in range(MLA_HEADS):
        kn = kvt_ref[hh * per_head:hh * per_head + MLA_NOPE, :]
        ss = jnp.sum(kn * kn, axis=0, keepdims=True) + ss_rope
        inv = lax.rsqrt(ss * (1.0 / MLA_QK) + EPS)
        kt = jnp.concatenate([kn * inv * g[:MLA_NOPE, :], r1 * inv, r2 * inv, pad], axis=0)
        k_ref[hh] = kt.T.astype(BF16)
        vt = kvt_ref[hh * per_head + MLA_NOPE:(hh + 1) * per_head, :]
        vt_ref[hh, 0] = jnp.concatenate([vt, ones_row], axis=0).astype(BF16)


def _mla_kv(x, pos, inv_freq, kv_norm, w_down, latent_norm, w_up, k_norm):
    s, d = x.shape
    rows = PROJ_ROWS
    w_dc = w_down[:, :MLA_KV_RANK].astype(BF16)
    w_drt = w_down[:, MLA_KV_RANK:].T.astype(BF16)
    w_upt = w_up.T.astype(BF16)
    n_up = w_upt.shape[0]
    kern = functools.partial(_kv_kernel, rows=rows)
    return pl.pallas_call(
        kern,
        out_shape=(jax.ShapeDtypeStruct((MLA_HEADS, s, QK_PAD), BF16),
                   jax.ShapeDtypeStruct((MLA_HEADS, s // rows, V_AUG, rows), BF16)),
        grid=(s // rows,),
        in_specs=[
            pl.BlockSpec((rows, d), lambda i: (i, 0)),
            _full((1, d)),
            _full((d, MLA_KV_RANK)),
            _full((MLA_ROPE, d)),
            _full((1, MLA_KV_RANK)),
            _full((n_up, MLA_KV_RANK)),
            _full((MLA_QK, 1)),
            pl.BlockSpec((1, rows), lambda i: (0, i)),
            _full((MLA_ROPE // 2, 1)),
        ],
        out_specs=(pl.BlockSpec((MLA_HEADS, rows, QK_PAD), lambda i: (0, i, 0)),
                   pl.BlockSpec((MLA_HEADS, 1, V_AUG, rows), lambda i: (0, i, 0, 0))),
        scratch_shapes=[pltpu.VMEM((n_up, rows), F32)],
        compiler_params=_params(("parallel",)),
        name="mla_kv",
    )(x, kv_norm.reshape(1, d), w_dc, w_drt, latent_norm.reshape(1, -1), w_upt,
      k_norm.reshape(-1, 1), pos, inv_freq)


def _q_kernel(x_ref, norm_ref, w_dq_ref, qlnorm_ref, w_uqt_ref, qnorm_ref, pos_ref, inv_freq_ref,
              qt_ref, qall_ref, *, rows):
    half = MLA_ROPE // 2
    h = _rms_rows(x_ref[...], norm_ref[...]).astype(BF16)
    c_q = _rms_rows(_dot(h, w_dq_ref[...]), qlnorm_ref[...]).astype(BF16)
    qall_ref[...] = _dot_nt(w_uqt_ref[...], c_q)
    cos, sin = _rope_tables(pos_ref, inv_freq_ref)
    g = qnorm_ref[...]
    scale = MLA_QK ** -0.5 * LOG2_E
    pad = jnp.zeros((QK_PAD - MLA_QK, rows), F32)
    for hh in range(MLA_HEADS):
        q = qall_ref[hh * MLA_QK:(hh + 1) * MLA_QK, :]
        inv = lax.rsqrt(jnp.mean(q * q, axis=0, keepdims=True) + EPS)
        qn = q * inv * g
        x1, x2 = qn[MLA_NOPE:MLA_NOPE + half, :], qn[MLA_NOPE + half:, :]
        qt = jnp.concatenate([qn[:MLA_NOPE, :], x1 * cos - x2 * sin, x1 * sin + x2 * cos, pad],
                             axis=0)
        qt_ref[hh] = (qt * scale).astype(BF16)


def _mla_q(x, pos, inv_freq, norm, w_dq, q_latent_norm, w_uq, q_norm):
    s, d = x.shape
    rows = PROJ_ROWS
    rank = w_dq.shape[1]
    w_uqt = w_uq.T.astype(BF16)
    kern = functools.partial(_q_kernel, rows=rows)
    return pl.pallas_call(
        kern,
        out_shape=jax.ShapeDtypeStruct((MLA_HEADS, QK_PAD, s), BF16),
        grid=(s // rows,),
        in_specs=[
            pl.BlockSpec((rows, d), lambda i: (i, 0)),
            _full((1, d)),
            _full((d, rank)),
            _full((1, rank)),
            _full(w_uqt.shape),
            _full((MLA_QK, 1)),
            pl.BlockSpec((1, rows), lambda i: (0, i)),
            _full((MLA_ROPE // 2, 1)),
        ],
        out_specs=pl.BlockSpec((MLA_HEADS, QK_PAD, rows), lambda i: (0, 0, i)),
        scratch_shapes=[pltpu.VMEM(w_uqt.shape[:1] + (rows,), F32)],
        compiler_params=_params(("parallel",)),
        name="mla_q",
    )(x, norm.reshape(1, d), w_dq.astype(BF16), q_latent_norm.reshape(1, -1), w_uqt,
      q_norm.reshape(-1, 1), pos, inv_freq)


def _attn_kernel(bounded_ref, qt_ref, k_ref, vt_ref, o_ref, acc_ref, p0_ref, p1_ref, *, tq, tk):
    i = pl.program_id(1)
    per_q = tq // tk
    assert per_q == 2
    key_c = lax.broadcasted_iota(jnp.int32, (tk, tk), 0) >> 6
    qry_c = lax.broadcasted_iota(jnp.int32, (tk, tk), 1) >> 6
    diag_mask = key_c <= qry_c

    def k_tile(j):
        return k_ref[0, pl.ds(pl.multiple_of(j * tk, tk), tk), :]

    @pl.when(bounded_ref[0] != 0)
    def _():
        qt = qt_ref[0]

        def probs(j):
            return jnp.exp2(_dot(k_tile(j), qt)).astype(BF16)

        def weighted_values(p_ref, j):
            return _dot(vt_ref[0, j], p_ref[...])

        j0, j1 = 2 * i, 2 * i + 1
        p = jnp.exp2(_dot(k_tile(j1), qt[:, tk:]))
        p1_ref[...] = jnp.concatenate([jnp.zeros((tk, tk), F32), jnp.where(diag_mask, p, 0.0)],
                                      axis=1).astype(BF16)
        acc_ref[...] = jnp.zeros_like(acc_ref)

        def tile_pairs(t, n_pairs):
            half = tk // 2
            tot = None
            for u in range(n_pairs):
                prev = 2 * (t + u) - 1
                if u == 0:
                    prev = jnp.where(t == 0, j1, prev)
                stages = ((p0_ref, 2 * (t + u), p1_ref, prev),
                          (p1_ref, 2 * (t + u) + 1, p0_ref, 2 * (t + u)))
                for new_ref, j_new, old_ref, j_old in stages:
                    for r0 in (0, half):
                        k_half = k_ref[0, pl.ds(pl.multiple_of(j_new * tk + r0, half), half), :]
                        p_new = jnp.exp2(_dot(k_half, qt)).astype(BF16)
                        part = _dot(vt_ref[0, j_old][:, r0:r0 + half], old_ref[r0:r0 + half, :])
                        tot = part if tot is None else tot + part
                        new_ref[r0:r0 + half, :] = p_new
            acc_ref[...] += tot

        done = 0
        for n_pairs, trips in ((4, i >> 2), (2, (i >> 1) & 1), (1, i & 1)):
            def body(u, carry, n_pairs=n_pairs, done=done):
                tile_pairs(done + n_pairs * u, n_pairs)
                return carry

            lax.fori_loop(0, trips, body, 0)
            done = done + n_pairs * trips

        p = jnp.exp2(_dot(k_tile(j0), qt))
        p = jnp.concatenate([jnp.where(diag_mask, p[:, :tk], 0.0), p[:, tk:]], axis=1).astype(BF16)
        acc = (acc_ref[...] + weighted_values(p1_ref, jnp.where(i == 0, j1, 2 * i - 1))
               + _dot(vt_ref[0, j0], p))
        o_ref[...] = (acc[:MLA_V, :] / acc[MLA_V:MLA_V + 1, :]).astype(BF16)

    @pl.when(bounded_ref[0] == 0)
    def _():
        qt = qt_ref[0]

        def step(j, carry, mask):
            m, l, acc = carry
            s = _dot(k_tile(j), qt)
            if mask is not None:
                s = jnp.where(mask, s, -jnp.inf)
            m_new = jnp.maximum(m, jnp.max(s, axis=0, keepdims=True))
            alpha = jnp.exp2(m - m_new)
            p = jnp.exp2(s - m_new)
            l = alpha * l + jnp.sum(p, axis=0, keepdims=True)
            acc = alpha * acc + _dot(vt_ref[0, j][:MLA_V, :], p.astype(BF16))
            return m_new, l, acc

        carry = (jnp.full((1, tq), -jnp.inf, F32), jnp.zeros((1, tq), F32),
                 jnp.zeros((MLA_V, tq), F32))
        carry = lax.fori_loop(0, i * per_q, lambda j, c: step(j, c, None), carry)
        key_w = lax.broadcasted_iota(jnp.int32, (tk, tq), 0) >> 6
        qry_w = lax.broadcasted_iota(jnp.int32, (tk, tq), 1) >> 6
        for jj in range(per_q):
            carry = step(i * per_q + jj, carry, (key_w + jj * (tk // CHUNK)) <= qry_w)
        _, l, acc = carry
        o_ref[...] = (acc / l).astype(BF16)


def _mla_attention(bounded, qt, k, vt):
    heads, _, s = qt.shape
    tq, tk = ATTN_TQ, ATTN_TK
    kern = functools.partial(_attn_kernel, tq=tq, tk=tk)
    return pl.pallas_call(
        kern,
        out_shape=jax.ShapeDtypeStruct((heads * MLA_V, s), BF16),
        grid_spec=pltpu.PrefetchScalarGridSpec(
            num_scalar_prefetch=1,
            grid=(heads, s // tq),
            in_specs=[
                pl.BlockSpec((1, QK_PAD, tq), lambda h, i, b: (h, 0, i)),
                pl.BlockSpec((1, s, QK_PAD), lambda h, i, b: (h, 0, 0)),
                pl.BlockSpec((1, s // tk, V_AUG, tk), lambda h, i, b: (h, 0, 0, 0)),
            ],
            out_specs=pl.BlockSpec((MLA_V, tq), lambda h, i, b: (h, i)),
            scratch_shapes=[pltpu.VMEM((V_AUG, tq), F32), pltpu.VMEM((tk, tq), BF16),
                            pltpu.VMEM((tk, tq), BF16)],
        ),
        compiler_params=_params(("parallel", "arbitrary")),
        name="mla_attn",
    )(bounded, qt, k, vt)


def kernel(x, positions, a_norm, a_w_in, a_w_gate_up, a_b_gate, a_out_norm, a_w_out, b_norm, b_w_dq,
           b_q_latent_norm, b_w_uq, b_q_norm, b_w_out, kv_norm, kv_w_down, kv_latent_norm, kv_w_up,
           k_norm, f_norm, f_w_in, f_w_out):
    batch, s, d = x.shape
    n_a = a_norm.shape[0]
    n_b = b_norm.shape[0]
    half = MLA_ROPE // 2
    inv_freq = (ROPE_THETA ** (-jnp.arange(half, dtype=F32) / half)).reshape(half, 1)
    outs = []
    for bi in range(batch):
        xb = x[bi]
        pos = positions[bi].reshape(1, s)
        k_sh = vt_sh = None
        for layer in range(n_a + n_b):
            if layer < n_a:
                xb = _gla_layer(xb, a_norm[layer], a_w_in[layer], a_w_gate_up[layer],
                                a_b_gate[layer], a_out_norm[layer], a_w_out[layer])
                xb = _ffn(xb, f_norm[layer], f_w_in[layer], f_w_out[layer])
            else:
                j = layer - n_a
                qt = _mla_q(xb, pos, inv_freq, b_norm[j], b_w_dq[j], b_q_latent_norm[j],
                            b_w_uq[j], b_q_norm[j])
                logit_bound = (LOGIT_BOUND_COEF * jnp.max(jnp.abs(b_q_norm[j]))
                               * jnp.max(jnp.abs(k_norm)))
                bounded = (logit_bound <= MAX_BOUNDED_LOGIT).astype(jnp.int32).reshape(1)
                attn_t = _mla_attention(bounded, qt, k_sh, vt_sh)
                xb = _ffn(xb, f_norm[layer], f_w_in[layer], f_w_out[layer],
                          attn_t=attn_t, w_o=b_w_out[j])
            if layer == n_a - 1 and n_b > 0:
                k_sh, vt_sh = _mla_kv(xb, pos, inv_freq, kv_norm, kv_w_down, kv_latent_norm,
                                      kv_w_up, k_norm)
        outs.append(xb)
    return jnp.stack(outs, axis=0)
```

```python
import functools
import itertools

import jax
import jax.numpy as jnp
from jax import lax
from jax.experimental import pallas as pl
from jax.experimental.pallas import tpu as pltpu

F32 = jnp.float32
BF16 = jnp.bfloat16

EPS = 1e-6
CHUNK = 64
GLA_HEADS = 4
GLA_TAU = 16.0
GLA_GATE_RANK = 16
MLA_HEADS = 16
MLA_NOPE = 64
MLA_ROPE = 32
MLA_QK = MLA_NOPE + MLA_ROPE
MLA_V = 64
MLA_KV_RANK = 256
ROPE_THETA = 10000.0
LOG2_E = 1.4426950408889634

MXU_COLS = 256
LANES = 128
SUBLANES = 8
QK_PAD = LANES
LOGIT_BOUND_COEF = 1.02 * LOG2_E * MLA_QK ** 0.5
MAX_BOUNDED_LOGIT = 100.0
V7X_VMEM_LIMIT_BYTES = 56 * 1024 * 1024

GLA_SUB_ROWS = 256
GLA_ROWS = 2 * GLA_SUB_ROWS
FFN_ROWS = 256
ATTN_TQ = 1024
ATTN_TK = ATTN_TQ // 2
PROJ_ROWS = ATTN_TK


def _dot(a, b):
    return jnp.dot(a, b, preferred_element_type=F32)


def _dot_nt(a, b):
    return lax.dot_general(a, b, (((1,), (1,)), ((), ())), preferred_element_type=F32)


def _dot_tn(a, b):
    return lax.dot_general(a, b, (((0,), (0,)), ((), ())), preferred_element_type=F32)


def _rms_rows(x, g):
    return x * lax.rsqrt(jnp.mean(x * x, axis=-1, keepdims=True) + EPS) * g


def _silu(x):
    return x * jax.nn.sigmoid(x)


def _params(semantics):
    return pltpu.CompilerParams(dimension_semantics=semantics,
                                vmem_limit_bytes=V7X_VMEM_LIMIT_BYTES)


def _full(shape):
    return pl.BlockSpec(shape, lambda *_: (0,) * len(shape))


def _gla_kernel(x_ref, xn_ref, norm_ref, w_main_ref, w_gl_ref, w_gate_ref, b_gate_ref, onorm_ref,
                w_out_ref, out_ref, state_ref, *scratch, sub, dk, dv):
    hk, hv = dk // GLA_HEADS, dv // GLA_HEADS
    subs = [scratch[0:3], scratch[3:6]]

    @pl.when(pl.program_id(0) == 0)
    def _():
        state_ref[...] = jnp.zeros_like(state_ref)

    r_i = lax.broadcasted_iota(jnp.int32, (sub, sub), 0)
    c_i = lax.broadcasted_iota(jnp.int32, (sub, sub), 1)
    tri = ((c_i <= r_i) & ((r_i >> 6) == (c_i >> 6))).astype(BF16)
    qi = lax.broadcasted_iota(jnp.int32, (CHUNK, CHUNK), 0)
    kj = lax.broadcasted_iota(jnp.int32, (CHUNK, CHUNK), 1)
    causal = kj <= qi
    q_scale = hk ** -0.5

    def project_pieces(x_rows, a):
        proj_ref, b_ref, _ = subs[a]
        cell = {}

        def norm():
            cell["h"] = _rms_rows(x_rows[...], norm_ref[...]).astype(BF16)

        def main_block(c0):
            def run():
                proj_ref[:, c0:c0 + MXU_COLS] = _dot(cell["h"], w_main_ref[:, c0:c0 + MXU_COLS])
            return run

        def gate_logits():
            gl = _dot(cell["h"], w_gl_ref[...]).astype(BF16)
            z = _dot(gl, w_gate_ref[...]) + b_gate_ref[...]
            cell["log_a"] = ((jnp.minimum(z, 0.0) - jnp.log1p(jnp.exp(-jnp.abs(z))))
                             * (1.0 / GLA_TAU))

        def cumulate():
            log_a = cell["log_a"]
            hi = log_a.astype(BF16)
            rest = log_a - hi.astype(F32)
            mid = rest.astype(BF16)
            lo = (rest - mid.astype(F32)).astype(BF16)
            b_ref[...] = _dot(tri, hi) + _dot(tri, mid) + _dot(tri, lo)

        blocks = [main_block(c0) for c0 in range(0, 2 * dk + 2 * dv, MXU_COLS)]
        return [norm, gate_logits] + blocks[:2] + [cumulate] + blocks[2:]

    def recur_pieces(a):
        proj_ref, b_ref, o_ref = subs[a]
        cell = {}

        def prepare(c):
            r0 = c * CHUNK
            b_c = b_ref[r0:r0 + CHUNK, :]
            b_last = b_c[CHUNK - 1:CHUNK, :]
            q_c = proj_ref[r0:r0 + CHUNK, 0:dk]
            k_c = proj_ref[r0:r0 + CHUNK, dk:2 * dk]
            cell[c] = dict(
                v=proj_ref[r0:r0 + CHUNK, 2 * dk:2 * dk + dv].astype(BF16),
                q_dec=(q_c * q_scale * jnp.exp(b_c)).astype(BF16),
                k_inv=(k_c * jnp.exp(-b_c)).astype(BF16),
                k_end=(k_c * jnp.exp(b_last - b_c)).astype(BF16),
                decay=jnp.exp(b_last))

        def unit(c, hh):
            def run():
                if hh == 0:
                    prepare(c)
                t = cell[c]
                r0 = c * CHUNK
                ks = slice(hh * hk, (hh + 1) * hk)
                vs = slice(hh * hv, (hh + 1) * hv)
                attn = jnp.where(causal, _dot_nt(t["q_dec"][:, ks], t["k_inv"][:, ks]),
                                 0.0).astype(BF16)
                state_t = state_ref[hh]
                o_h = (_dot(attn, t["v"][:, vs])
                       + _dot_nt(t["q_dec"][:, ks], state_t.astype(BF16)))
                state_ref[hh] = (t["decay"][:, ks] * state_t
                                 + _dot_tn(t["v"][:, vs], t["k_end"][:, ks]))
                o_ref[r0:r0 + CHUNK, vs] = _rms_rows(o_h, onorm_ref[...])
            return run

        return [unit(c, hh) for c in range(sub // CHUNK) for hh in range(GLA_HEADS)]

    def output_pieces(a):
        proj_ref, _, o_ref = subs[a]
        rows = slice(a * sub, (a + 1) * sub)
        cell = {}

        def gate():
            g = _silu(proj_ref[:, 2 * dk + dv:2 * dk + 2 * dv])
            cell["g"] = (o_ref[...] * g).astype(BF16)

        def out_block(c0):
            def run():
                cols = slice(c0, c0 + MXU_COLS)
                out_ref[rows, cols] = x_ref[rows, cols] + _dot(cell["g"], w_out_ref[:, cols])
            return run

        return [gate] + [out_block(c0) for c0 in range(0, out_ref.shape[1], MXU_COLS)]

    def interleave(first, second):
        for f, g in itertools.zip_longest(first, second):
            for step in (f, g):
                if step is not None:
                    step()

    @pl.when(pl.program_id(0) == 0)
    def _():
        interleave(project_pieces(x_ref.at[0:sub, :], 0), [])

    next_proj = project_pieces(xn_ref, 0)
    out0, out1 = output_pieces(0), output_pieces(1)
    split = len(next_proj) - len(out1) - 1
    interleave(recur_pieces(0), project_pieces(x_ref.at[sub:2 * sub, :], 1))
    interleave(recur_pieces(1), out0 + next_proj[:split])
    interleave(out1, next_proj[split:])


def _gla_layer(x, norm, w_in, w_gate_up, b_gate, out_norm, w_out):
    s, d = x.shape
    dk = w_gate_up.shape[1]
    dv = (w_in.shape[1] - 2 * dk - GLA_GATE_RANK) // 2
    n_main = 2 * dk + 2 * dv
    rows = GLA_ROWS
    w_main = w_in[:, :n_main].astype(BF16)
    w_gl = jnp.pad(w_in[:, n_main:], ((0, 0), (0, LANES - GLA_GATE_RANK))).astype(BF16)
    w_gate = jnp.pad(w_gate_up, ((0, LANES - GLA_GATE_RANK), (0, 0))).astype(BF16)
    sub = GLA_SUB_ROWS
    last_sub = s // sub - 1
    sub_scratch = [pltpu.VMEM((sub, n_main), F32), pltpu.VMEM((sub, dk), F32),
                   pltpu.VMEM((sub, dv), F32)]
    kern = functools.partial(_gla_kernel, sub=sub, dk=dk, dv=dv)
    return pl.pallas_call(
        kern,
        out_shape=jax.ShapeDtypeStruct((s, d), F32),
        grid=(s // rows,),
        in_specs=[
            pl.BlockSpec((rows, d), lambda i: (i, 0)),
            pl.BlockSpec((sub, d), lambda i: (jnp.minimum(2 * i + 2, last_sub), 0)),
            _full((1, d)),
            _full((d, n_main)),
            _full((d, LANES)),
            _full((LANES, dk)),
            _full((1, dk)),
            _full((1, dv // GLA_HEADS)),
            _full((dv, d)),
        ],
        out_specs=pl.BlockSpec((rows, d), lambda i: (i, 0)),
        scratch_shapes=[pltpu.VMEM((GLA_HEADS, dv // GLA_HEADS, dk // GLA_HEADS), F32)]
        + sub_scratch * 2,
        compiler_params=_params(("arbitrary",)),
        name="gla_layer",
    )(x, x, norm.reshape(1, d), w_main, w_gl, w_gate, b_gate.reshape(1, dk),
      out_norm.reshape(1, -1), w_out.astype(BF16))


def _ffn_body(x, norm_ref, w_in_ref, w_out_ref, out_ref, hidden):
    h = _rms_rows(x, norm_ref[...]).astype(BF16)
    gate = _dot(h, w_in_ref[:, 0:hidden])
    up = _dot(h, w_in_ref[:, hidden:2 * hidden])
    act = (_silu(gate) * up).astype(BF16)
    out_ref[...] = x + _dot(act, w_out_ref[...])


def _ffn_kernel(x_ref, norm_ref, w_in_ref, w_out_ref, out_ref, *, hidden):
    _ffn_body(x_ref[...], norm_ref, w_in_ref, w_out_ref, out_ref, hidden)


def _proj_ffn_kernel(x_ref, ot_ref, w_o_ref, norm_ref, w_in_ref, w_out_ref, out_ref, *, hidden):
    x = x_ref[...] + _dot_tn(ot_ref[...], w_o_ref[...])
    _ffn_body(x, norm_ref, w_in_ref, w_out_ref, out_ref, hidden)


def _ffn(x, norm, w_in, w_out, attn_t=None, w_o=None):
    s, d = x.shape
    hidden = w_out.shape[0]
    rows = FFN_ROWS
    row_spec = pl.BlockSpec((rows, d), lambda i: (i, 0))
    ffn_specs = [_full((1, d)), _full((d, 2 * hidden)), _full((hidden, d))]
    ffn_args = (norm.reshape(1, d), w_in.astype(BF16), w_out.astype(BF16))
    if attn_t is None:
        kern = functools.partial(_ffn_kernel, hidden=hidden)
        in_specs = [row_spec] + ffn_specs
        args = (x,) + ffn_args
        name = "ffn"
    else:
        kern = functools.partial(_proj_ffn_kernel, hidden=hidden)
        in_specs = [row_spec, pl.BlockSpec((attn_t.shape[0], rows), lambda i: (0, i)),
                    _full(w_o.shape)] + ffn_specs
        args = (x, attn_t, w_o.astype(BF16)) + ffn_args
        name = "attn_proj_ffn"
    return pl.pallas_call(
        kern,
        out_shape=jax.ShapeDtypeStruct((s, d), F32),
        grid=(s // rows,),
        in_specs=in_specs,
        out_specs=row_spec,
        compiler_params=_params(("parallel",)),
        name=name,
    )(*args)


def _rope_tables(pos_ref, inv_freq_ref):
    ang = pos_ref[...].astype(F32) * inv_freq_ref[...]
    return jnp.cos(ang), jnp.sin(ang)


def _kv_kernel(x_ref, norm_ref, w_dc_ref, w_drt_ref, lnorm_ref, w_upt_ref, knorm_ref, pos_ref,
               inv_freq_ref, k_ref, vt_ref, kvt_ref, *, rows):
    half = MLA_ROPE // 2
    h = _rms_rows(x_ref[...], norm_ref[...]).astype(BF16)
    c_kv = _rms_rows(_dot(h, w_dc_ref[...]), lnorm_ref[...]).astype(BF16)
    kvt_ref[...] = _dot_nt(w_upt_ref[...], c_kv)
    kr = _dot_nt(w_drt_ref[...], h)
    cos, sin = _rope_tables(pos_ref, inv_freq_ref)
    g = knorm_ref[...]
    ss_rope = jnp.sum(kr * kr, axis=0, keepdims=True)
    krg = kr * g[MLA_NOPE:, :]
    x1, x2 = krg[:half, :], krg[half:, :]
    r1 = x1 * cos - x2 * sin
    r2 = x1 * sin + x2 * cos
    pad = jnp.zeros((QK_PAD - MLA_QK, rows), F32)
    per_head = MLA_NOPE + MLA_V
    for hh in range(MLA_HEADS):
        kn = kvt_ref[hh * per_head:hh * per_head + MLA_NOPE, :]
        ss = jnp.sum(kn * kn, axis=0, keepdims=True) + ss_rope
        inv = lax.rsqrt(ss * (1.0 / MLA_QK) + EPS)
        kt = jnp.concatenate([kn * inv * g[:MLA_NOPE, :], r1 * inv, r2 * inv, pad], axis=0)
        k_ref[hh] = kt.T.astype(BF16)
        vt_ref[hh, 0] = kvt_ref[hh * per_head + MLA_NOPE:(hh + 1) * per_head, :].astype(BF16)


def _mla_kv(x, pos, inv_freq, kv_norm, w_down, latent_norm, w_up, k_norm):
    s, d = x.shape
    rows = PROJ_ROWS
    w_dc = w_down[:, :MLA_KV_RANK].astype(BF16)
    w_drt = w_down[:, MLA_KV_RANK:].T.astype(BF16)
    w_upt = w_up.T.astype(BF16)
    n_up = w_upt.shape[0]
    kern = functools.partial(_kv_kernel, rows=rows)
    return pl.pallas_call(
        kern,
        out_shape=(jax.ShapeDtypeStruct((MLA_HEADS, s, QK_PAD), BF16),
                   jax.ShapeDtypeStruct((MLA_HEADS, s // rows, MLA_V, rows), BF16)),
        grid=(s // rows,),
        in_specs=[
            pl.BlockSpec((rows, d), lambda i: (i, 0)),
            _full((1, d)),
            _full((d, MLA_KV_RANK)),
            _full((MLA_ROPE, d)),
            _full((1, MLA_KV_RANK)),
            _full((n_up, MLA_KV_RANK)),
            _full((MLA_QK, 1)),
            pl.BlockSpec((1, rows), lambda i: (0, i)),
            _full((MLA_ROPE // 2, 1)),
        ],
        out_specs=(pl.BlockSpec((MLA_HEADS, rows, QK_PAD), lambda i: (0, i, 0)),
                   pl.BlockSpec((MLA_HEADS, 1, MLA_V, rows), lambda i: (0, i, 0, 0))),
        scratch_shapes=[pltpu.VMEM((n_up, rows), F32)],
        compiler_params=_params(("parallel",)),
        name="mla_kv",
    )(x, kv_norm.reshape(1, d), w_dc, w_drt, latent_norm.reshape(1, -1), w_upt,
      k_norm.reshape(-1, 1), pos, inv_freq)


def _q_kernel(x_ref, norm_ref, w_dq_ref, qlnorm_ref, w_uqt_ref, qnorm_ref, pos_ref, inv_freq_ref,
              qt_ref, qall_ref, *, rows):
    half = MLA_ROPE // 2
    h = _rms_rows(x_ref[...], norm_ref[...]).astype(BF16)
    c_q = _rms_rows(_dot(h, w_dq_ref[...]), qlnorm_ref[...]).astype(BF16)
    qall_ref[...] = _dot_nt(w_uqt_ref[...], c_q)
    cos, sin = _rope_tables(pos_ref, inv_freq_ref)
    g = qnorm_ref[...]
    scale = MLA_QK ** -0.5 * LOG2_E
    pad = jnp.zeros((QK_PAD - MLA_QK, rows), F32)
    for hh in range(MLA_HEADS):
        q = qall_ref[hh * MLA_QK:(hh + 1) * MLA_QK, :]
        inv = lax.rsqrt(jnp.mean(q * q, axis=0, keepdims=True) + EPS)
        qn = q * inv * g
        x1, x2 = qn[MLA_NOPE:MLA_NOPE + half, :], qn[MLA_NOPE + half:, :]
        qt = jnp.concatenate([qn[:MLA_NOPE, :], x1 * cos - x2 * sin, x1 * sin + x2 * cos, pad],
                             axis=0)
        qt_ref[hh] = (qt * scale).astype(BF16)


def _mla_q(x, pos, inv_freq, norm, w_dq, q_latent_norm, w_uq, q_norm):
    s, d = x.shape
    rows = PROJ_ROWS
    rank = w_dq.shape[1]
    w_uqt = w_uq.T.astype(BF16)
    kern = functools.partial(_q_kernel, rows=rows)
    return pl.pallas_call(
        kern,
        out_shape=jax.ShapeDtypeStruct((MLA_HEADS, QK_PAD, s), BF16),
        grid=(s // rows,),
        in_specs=[
            pl.BlockSpec((rows, d), lambda i: (i, 0)),
            _full((1, d)),
            _full((d, rank)),
            _full((1, rank)),
            _full(w_uqt.shape),
            _full((MLA_QK, 1)),
            pl.BlockSpec((1, rows), lambda i: (0, i)),
            _full((MLA_ROPE // 2, 1)),
        ],
        out_specs=pl.BlockSpec((MLA_HEADS, QK_PAD, rows), lambda i: (0, 0, i)),
        scratch_shapes=[pltpu.VMEM(w_uqt.shape[:1] + (rows,), F32)],
        compiler_params=_params(("parallel",)),
        name="mla_q",
    )(x, norm.reshape(1, d), w_dq.astype(BF16), q_latent_norm.reshape(1, -1), w_uqt,
      q_norm.reshape(-1, 1), pos, inv_freq)


def _attn_kernel(bounded_ref, qt_ref, k_ref, vt_ref, o_ref, acc_ref, l_ref, p0_ref, p1_ref, *,
                 tq, tk):
    i = pl.program_id(1)
    per_q = tq // tk
    assert per_q == 2
    key_c = lax.broadcasted_iota(jnp.int32, (tk, tk), 0) >> 6
    qry_c = lax.broadcasted_iota(jnp.int32, (tk, tk), 1) >> 6
    diag_mask = key_c <= qry_c

    def k_tile(j):
        return k_ref[0, pl.ds(pl.multiple_of(j * tk, tk), tk), :]

    @pl.when(bounded_ref[0] != 0)
    def _():
        qt = qt_ref[0]

        def sublane_sums(p):
            return p.reshape(p.shape[0] // SUBLANES, SUBLANES, p.shape[1]).sum(axis=0)

        def weighted_values(p_ref, j):
            return _dot(vt_ref[0, j], p_ref[...])

        j0, j1 = 2 * i, 2 * i + 1
        p = jnp.where(diag_mask, jnp.exp2(_dot(k_tile(j1), qt[:, tk:])), 0.0)
        p1_ref[...] = jnp.concatenate([jnp.zeros((tk, tk), F32), p], axis=1).astype(BF16)
        l_ref[...] = jnp.concatenate([jnp.zeros((SUBLANES, tk), F32), sublane_sums(p)], axis=1)
        acc_ref[...] = jnp.zeros_like(acc_ref)

        def tile_pairs(t, n_pairs):
            half = tk // 2
            tot = l_tot = None
            for u in range(n_pairs):
                prev = 2 * (t + u) - 1
                if u == 0:
                    prev = jnp.where(t == 0, j1, prev)
                stages = ((p0_ref, 2 * (t + u), p1_ref, prev),
                          (p1_ref, 2 * (t + u) + 1, p0_ref, 2 * (t + u)))
                for new_ref, j_new, old_ref, j_old in stages:
                    for r0 in (0, half):
                        k_half = k_ref[0, pl.ds(pl.multiple_of(j_new * tk + r0, half), half), :]
                        p_new = jnp.exp2(_dot(k_half, qt))
                        l_part = sublane_sums(p_new)
                        part = _dot(vt_ref[0, j_old][:, r0:r0 + half], old_ref[r0:r0 + half, :])
                        tot = part if tot is None else tot + part
                        l_tot = l_part if l_tot is None else l_tot + l_part
                        new_ref[r0:r0 + half, :] = p_new.astype(BF16)
            acc_ref[...] += tot
            l_ref[...] += l_tot

        done = 0
        for n_pairs, trips in ((4, i >> 2), (2, (i >> 1) & 1), (1, i & 1)):
            def body(u, carry, n_pairs=n_pairs, done=done):
                tile_pairs(done + n_pairs * u, n_pairs)
                return carry

            lax.fori_loop(0, trips, body, 0)
            done = done + n_pairs * trips

        p = jnp.exp2(_dot(k_tile(j0), qt))
        p = jnp.concatenate([jnp.where(diag_mask, p[:, :tk], 0.0), p[:, tk:]], axis=1)
        acc = (acc_ref[...] + weighted_values(p1_ref, jnp.where(i == 0, j1, 2 * i - 1))
               + _dot(vt_ref[0, j0], p.astype(BF16)))
        l = jnp.sum(l_ref[...] + sublane_sums(p), axis=0, keepdims=True)
        o_ref[...] = (acc / l).astype(BF16)

    @pl.when(bounded_ref[0] == 0)
    def _():
        qt = qt_ref[0]

        def step(j, carry, mask):
            m, l, acc = carry
            s = _dot(k_tile(j), qt)
            if mask is not None:
                s = jnp.where(mask, s, -jnp.inf)
            m_new = jnp.maximum(m, jnp.max(s, axis=0, keepdims=True))
            alpha = jnp.exp2(m - m_new)
            p = jnp.exp2(s - m_new)
            l = alpha * l + jnp.sum(p, axis=0, keepdims=True)
            acc = alpha * acc + _dot(vt_ref[0, j], p.astype(BF16))
            return m_new, l, acc

        carry = (jnp.full((1, tq), -jnp.inf, F32), jnp.zeros((1, tq), F32),
                 jnp.zeros((MLA_V, tq), F32))
        carry = lax.fori_loop(0, i * per_q, lambda j, c: step(j, c, None), carry)
        key_w = lax.broadcasted_iota(jnp.int32, (tk, tq), 0) >> 6
        qry_w = lax.broadcasted_iota(jnp.int32, (tk, tq), 1) >> 6
        for jj in range(per_q):
            carry = step(i * per_q + jj, carry, (key_w + jj * (tk // CHUNK)) <= qry_w)
        _, l, acc = carry
        o_ref[...] = (acc / l).astype(BF16)


def _mla_attention(bounded, qt, k, vt):
    heads, _, s = qt.shape
    tq, tk = ATTN_TQ, ATTN_TK
    kern = functools.partial(_attn_kernel, tq=tq, tk=tk)
    return pl.pallas_call(
        kern,
        out_shape=jax.ShapeDtypeStruct((heads * MLA_V, s), BF16),
        grid_spec=pltpu.PrefetchScalarGridSpec(
            num_scalar_prefetch=1,
            grid=(heads, s // tq),
            in_specs=[
                pl.BlockSpec((1, QK_PAD, tq), lambda h, i, b: (h, 0, i)),
                pl.BlockSpec((1, s, QK_PAD), lambda h, i, b: (h, 0, 0)),
                pl.BlockSpec((1, s // tk, MLA_V, tk), lambda h, i, b: (h, 0, 0, 0)),
            ],
            out_specs=pl.BlockSpec((MLA_V, tq), lambda h, i, b: (h, i)),
            scratch_shapes=[pltpu.VMEM((MLA_V, tq), F32), pltpu.VMEM((SUBLANES, tq), F32),
                            pltpu.VMEM((tk, tq), BF16), pltpu.VMEM((tk, tq), BF16)],
        ),
        compiler_params=_params(("parallel", "arbitrary")),
        name="mla_attn",
    )(bounded, qt, k, vt)


def kernel(x, positions, a_norm, a_w_in, a_w_gate_up, a_b_gate, a_out_norm, a_w_out, b_norm, b_w_dq,
           b_q_latent_norm, b_w_uq, b_q_norm, b_w_out, kv_norm, kv_w_down, kv_latent_norm, kv_w_up,
           k_norm, f_norm, f_w_in, f_w_out):
    batch, s, d = x.shape
    n_a = a_norm.shape[0]
    n_b = b_norm.shape[0]
    half = MLA_ROPE // 2
    inv_freq = (ROPE_THETA ** (-jnp.arange(half, dtype=F32) / half)).reshape(half, 1)
    outs = []
    for bi in range(batch):
        xb = x[bi]
        pos = positions[bi].reshape(1, s)
        k_sh = vt_sh = None
        for layer in range(n_a + n_b):
            if layer < n_a:
                xb = _gla_layer(xb, a_norm[layer], a_w_in[layer], a_w_gate_up[layer],
                                a_b_gate[layer], a_out_norm[layer], a_w_out[layer])
                xb = _ffn(xb, f_norm[layer], f_w_in[layer], f_w_out[layer])
            else:
                j = layer - n_a
                qt = _mla_q(xb, pos, inv_freq, b_norm[j], b_w_dq[j], b_q_latent_norm[j],
                            b_w_uq[j], b_q_norm[j])
                logit_bound = (LOGIT_BOUND_COEF * jnp.max(jnp.abs(b_q_norm[j]))
                               * jnp.max(jnp.abs(k_norm)))
                bounded = (logit_bound <= MAX_BOUNDED_LOGIT).astype(jnp.int32).reshape(1)
                attn_t = _mla_attention(bounded, qt, k_sh, vt_sh)
                xb = _ffn(xb, f_norm[layer], f_w_in[layer], f_w_out[layer],
                          attn_t=attn_t, w_o=b_w_out[j])
            if layer == n_a - 1 and n_b > 0:
                k_sh, vt_sh = _mla_kv(xb, pos, inv_freq, kv_norm, kv_w_down, kv_latent_norm,
                                      kv_w_up, k_norm)
        outs.append(xb)
    return jnp.stack(outs, axis=0)
```

```python
import functools
import itertools

import jax
import jax.numpy as jnp
from jax import lax
from jax.experimental import pallas as pl
from jax.experimental.pallas import tpu as pltpu

F32 = jnp.float32
BF16 = jnp.bfloat16

EPS = 1e-6
CHUNK = 64
GLA_HEADS = 4
GLA_TAU = 16.0
GLA_GATE_RANK = 16
MLA_HEADS = 16
MLA_NOPE = 64
MLA_ROPE = 32
MLA_QK = MLA_NOPE + MLA_ROPE
MLA_V = 64
MLA_KV_RANK = 256
ROPE_THETA = 10000.0
LOG2_E = 1.4426950408889634

MXU_COLS = 256
LANES = 128
SUBLANES = 8
QK_PAD = LANES
LOGIT_BOUND_COEF = 1.02 * LOG2_E * MLA_QK ** 0.5
MAX_BOUNDED_LOGIT = 100.0
V7X_VMEM_LIMIT_BYTES = 56 * 1024 * 1024

GLA_SUB_ROWS = 256
GLA_ROWS = 2 * GLA_SUB_ROWS
FFN_ROWS = 512
ATTN_TQ = 1024
ATTN_TK = ATTN_TQ // 2
PROJ_ROWS = ATTN_TK


def _dot(a, b):
    return jnp.dot(a, b, preferred_element_type=F32)


def _dot_nt(a, b):
    return lax.dot_general(a, b, (((1,), (1,)), ((), ())), preferred_element_type=F32)


def _dot_tn(a, b):
    return lax.dot_general(a, b, (((0,), (0,)), ((), ())), preferred_element_type=F32)


def _rms_rows(x, g):
    return x * lax.rsqrt(jnp.mean(x * x, axis=-1, keepdims=True) + EPS) * g


def _silu(x):
    return x * jax.nn.sigmoid(x)


def _params(semantics):
    return pltpu.CompilerParams(dimension_semantics=semantics,
                                vmem_limit_bytes=V7X_VMEM_LIMIT_BYTES)


def _interleave(first, second):
    for f, g in itertools.zip_longest(first, second):
        for step in (f, g):
            if step is not None:
                step()


def _full(shape):
    return pl.BlockSpec(shape, lambda *_: (0,) * len(shape), pipeline_mode=pl.Buffered(1))


def _gla_kernel(x_ref, xn_ref, norm_ref, w_main_ref, w_gl_ref, w_gate_ref, b_gate_ref, onorm_ref,
                w_out_ref, out_ref, state_ref, *scratch, sub, dk, dv):
    hk, hv = dk // GLA_HEADS, dv // GLA_HEADS
    subs = [scratch[0:3], scratch[3:6]]

    @pl.when(pl.program_id(0) == 0)
    def _():
        state_ref[...] = jnp.zeros_like(state_ref)

    r_i = lax.broadcasted_iota(jnp.int32, (sub, sub), 0)
    c_i = lax.broadcasted_iota(jnp.int32, (sub, sub), 1)
    tri = ((c_i <= r_i) & ((r_i >> 6) == (c_i >> 6))).astype(BF16)
    qi = lax.broadcasted_iota(jnp.int32, (CHUNK, CHUNK), 0)
    kj = lax.broadcasted_iota(jnp.int32, (CHUNK, CHUNK), 1)
    causal = kj <= qi
    q_scale = hk ** -0.5

    def project_pieces(x_rows, a):
        proj_ref, b_ref, _ = subs[a]
        cell = {}

        def norm():
            cell["h"] = _rms_rows(x_rows[...], norm_ref[...]).astype(BF16)

        def main_block(c0):
            def run():
                proj_ref[:, c0:c0 + MXU_COLS] = _dot(cell["h"], w_main_ref[:, c0:c0 + MXU_COLS])
            return run

        def gate_logits():
            gl = _dot(cell["h"], w_gl_ref[...]).astype(BF16)
            z = _dot(gl, w_gate_ref[...]) + b_gate_ref[...]
            cell["log_a"] = ((jnp.minimum(z, 0.0) - jnp.log1p(jnp.exp(-jnp.abs(z))))
                             * (1.0 / GLA_TAU))

        def cumulate():
            log_a = cell["log_a"]
            hi = log_a.astype(BF16)
            rest = log_a - hi.astype(F32)
            mid = rest.astype(BF16)
            lo = (rest - mid.astype(F32)).astype(BF16)
            b_ref[...] = _dot(tri, hi) + _dot(tri, mid) + _dot(tri, lo)

        blocks = [main_block(c0) for c0 in range(0, 2 * dk + 2 * dv, MXU_COLS)]
        return [norm, gate_logits] + blocks[:2] + [cumulate] + blocks[2:]

    def recur_pieces(a):
        proj_ref, b_ref, o_ref = subs[a]
        cell = {}

        def prepare(c):
            r0 = c * CHUNK
            b_c = b_ref[r0:r0 + CHUNK, :]
            b_last = b_c[CHUNK - 1:CHUNK, :]
            q_c = proj_ref[r0:r0 + CHUNK, 0:dk]
            k_c = proj_ref[r0:r0 + CHUNK, dk:2 * dk]
            cell[c] = dict(
                v=proj_ref[r0:r0 + CHUNK, 2 * dk:2 * dk + dv].astype(BF16),
                q_dec=(q_c * q_scale * jnp.exp(b_c)).astype(BF16),
                k_inv=(k_c * jnp.exp(-b_c)).astype(BF16),
                k_end=(k_c * jnp.exp(b_last - b_c)).astype(BF16),
                decay=jnp.exp(b_last))

        def unit(c, hh):
            def run():
                if hh == 0:
                    prepare(c)
                t = cell[c]
                r0 = c * CHUNK
                ks = slice(hh * hk, (hh + 1) * hk)
                vs = slice(hh * hv, (hh + 1) * hv)
                attn = jnp.where(causal, _dot_nt(t["q_dec"][:, ks], t["k_inv"][:, ks]),
                                 0.0).astype(BF16)
                state_t = state_ref[hh]
                o_h = (_dot(attn, t["v"][:, vs])
                       + _dot_nt(t["q_dec"][:, ks], state_t.astype(BF16)))
                state_ref[hh] = (t["decay"][:, ks] * state_t
                                 + _dot_tn(t["v"][:, vs], t["k_end"][:, ks]))
                o_ref[r0:r0 + CHUNK, vs] = _rms_rows(o_h, onorm_ref[...])
            return run

        return [unit(c, hh) for c in range(sub // CHUNK) for hh in range(GLA_HEADS)]

    def output_pieces(a):
        proj_ref, _, o_ref = subs[a]
        rows = slice(a * sub, (a + 1) * sub)
        cell = {}

        def gate():
            g = _silu(proj_ref[:, 2 * dk + dv:2 * dk + 2 * dv])
            cell["g"] = (o_ref[...] * g).astype(BF16)

        def out_block(c0):
            def run():
                cols = slice(c0, c0 + MXU_COLS)
                out_ref[rows, cols] = x_ref[rows, cols] + _dot(cell["g"], w_out_ref[:, cols])
            return run

        return [gate] + [out_block(c0) for c0 in range(0, out_ref.shape[1], MXU_COLS)]

    @pl.when(pl.program_id(0) == 0)
    def _():
        _interleave(project_pieces(x_ref.at[0:sub, :], 0), [])

    next_proj = project_pieces(xn_ref, 0)
    out0, out1 = output_pieces(0), output_pieces(1)
    split = len(next_proj) - len(out1) - 1
    _interleave(recur_pieces(0), project_pieces(x_ref.at[sub:2 * sub, :], 1))
    _interleave(recur_pieces(1), out0 + next_proj[:split])
    _interleave(out1, next_proj[split:])


def _gla_layer(x, norm, w_in, w_gate_up, b_gate, out_norm, w_out):
    s, d = x.shape
    dk = w_gate_up.shape[1]
    dv = (w_in.shape[1] - 2 * dk - GLA_GATE_RANK) // 2
    n_main = 2 * dk + 2 * dv
    rows = GLA_ROWS
    w_main = w_in[:, :n_main].astype(BF16)
    w_gl = jnp.pad(w_in[:, n_main:], ((0, 0), (0, LANES - GLA_GATE_RANK))).astype(BF16)
    w_gate = jnp.pad(w_gate_up, ((0, LANES - GLA_GATE_RANK), (0, 0))).astype(BF16)
    sub = GLA_SUB_ROWS
    last_sub = s // sub - 1
    sub_scratch = [pltpu.VMEM((sub, n_main), F32), pltpu.VMEM((sub, dk), F32),
                   pltpu.VMEM((sub, dv), F32)]
    kern = functools.partial(_gla_kernel, sub=sub, dk=dk, dv=dv)
    return pl.pallas_call(
        kern,
        out_shape=jax.ShapeDtypeStruct((s, d), F32),
        grid=(s // rows,),
        in_specs=[
            pl.BlockSpec((rows, d), lambda i: (i, 0)),
            pl.BlockSpec((sub, d), lambda i: (jnp.minimum(2 * i + 2, last_sub), 0)),
            _full((1, d)),
            _full((d, n_main)),
            _full((d, LANES)),
            _full((LANES, dk)),
            _full((1, dk)),
            _full((1, dv // GLA_HEADS)),
            _full((dv, d)),
        ],
        out_specs=pl.BlockSpec((rows, d), lambda i: (i, 0)),
        scratch_shapes=[pltpu.VMEM((GLA_HEADS, dv // GLA_HEADS, dk // GLA_HEADS), F32)]
        + sub_scratch * 2,
        compiler_params=_params(("arbitrary",)),
        name="gla_layer",
    )(x, x, norm.reshape(1, d), w_main, w_gl, w_gate, b_gate.reshape(1, dk),
      out_norm.reshape(1, -1), w_out.astype(BF16))


def _ffn_body(x, norm_ref, w_in_ref, w_out_ref, out_ref, hidden):
    h = _rms_rows(x, norm_ref[...]).astype(BF16)
    gate = _dot(h, w_in_ref[:, 0:hidden])
    up = _dot(h, w_in_ref[:, hidden:2 * hidden])
    act = (_silu(gate) * up).astype(BF16)
    out_ref[...] = x + _dot(act, w_out_ref[...])


def _ffn_kernel(x_ref, norm_ref, w_in_ref, w_out_ref, out_ref, *, hidden):
    _ffn_body(x_ref[...], norm_ref, w_in_ref, w_out_ref, out_ref, hidden)


def _proj_ffn_kernel(x_ref, ot_ref, w_o_ref, norm_ref, w_in_ref, w_out_ref, out_ref, *, hidden):
    x = x_ref[...] + _dot_tn(ot_ref[...], w_o_ref[...])
    _ffn_body(x, norm_ref, w_in_ref, w_out_ref, out_ref, hidden)


def _ffn(x, norm, w_in, w_out, attn_t=None, w_o=None):
    s, d = x.shape
    hidden = w_out.shape[0]
    rows = FFN_ROWS
    row_spec = pl.BlockSpec((rows, d), lambda i: (i, 0))
    ffn_specs = [_full((1, d)), _full((d, 2 * hidden)), _full((hidden, d))]
    ffn_args = (norm.reshape(1, d), w_in.astype(BF16), w_out.astype(BF16))
    if attn_t is None:
        kern = functools.partial(_ffn_kernel, hidden=hidden)
        in_specs = [row_spec] + ffn_specs
        args = (x,) + ffn_args
        name = "ffn"
    else:
        kern = functools.partial(_proj_ffn_kernel, hidden=hidden)
        in_specs = [row_spec, pl.BlockSpec((attn_t.shape[0], rows), lambda i: (0, i)),
                    _full(w_o.shape)] + ffn_specs
        args = (x, attn_t, w_o.astype(BF16)) + ffn_args
        name = "attn_proj_ffn"
    return pl.pallas_call(
        kern,
        out_shape=jax.ShapeDtypeStruct((s, d), F32),
        grid=(s // rows,),
        in_specs=in_specs,
        out_specs=row_spec,
        compiler_params=_params(("parallel",)),
        name=name,
    )(*args)


def _qkv_kernel(*refs, rows, with_kv):
    if with_kv:
        (x_ref, pos_ref, inv_freq_ref, qn_ref, w_dq_ref, qln_ref, w_uqt_ref, qg_ref,
         kvn_ref, w_dc_ref, w_drt_ref, lnorm_ref, w_upt_ref, kg_ref,
         qt_ref, k_ref, vt_ref, qall_ref, kvt_ref) = refs
    else:
        (x_ref, pos_ref, inv_freq_ref, qn_ref, w_dq_ref, qln_ref, w_uqt_ref, qg_ref,
         qt_ref, qall_ref) = refs
    half = MLA_ROPE // 2
    group = 4
    x = x_ref[...]
    xn = x * lax.rsqrt(jnp.mean(x * x, axis=-1, keepdims=True) + EPS)
    ang = pos_ref[...].astype(F32) * inv_freq_ref[...]
    cos, sin = jnp.cos(ang), jnp.sin(ang)
    pad = jnp.zeros((QK_PAD - MLA_QK, rows), F32)
    cell = {}

    def q_latent():
        h = (xn * qn_ref[...]).astype(BF16)
        cell["c_q"] = _rms_rows(_dot(h, w_dq_ref[...]), qln_ref[...]).astype(BF16)

    def q_heads(h0):
        def run():
            r0, r1 = h0 * MLA_QK, (h0 + group) * MLA_QK
            qall_ref[r0:r1, :] = _dot_nt(w_uqt_ref[r0:r1, :], cell["c_q"])
            g = qg_ref[...]
            scale = MLA_QK ** -0.5 * LOG2_E
            for hh in range(h0, h0 + group):
                q = qall_ref[hh * MLA_QK:(hh + 1) * MLA_QK, :]
                inv = lax.rsqrt(jnp.mean(q * q, axis=0, keepdims=True) + EPS)
                qn = q * inv * g
                x1, x2 = qn[MLA_NOPE:MLA_NOPE + half, :], qn[MLA_NOPE + half:, :]
                qt = jnp.concatenate([qn[:MLA_NOPE, :], x1 * cos - x2 * sin,
                                      x1 * sin + x2 * cos, pad], axis=0)
                qt_ref[hh] = (qt * scale).astype(BF16)
        return run

    q_steps = [q_latent] + [q_heads(h0) for h0 in range(0, MLA_HEADS, group)]
    if not with_kv:
        _interleave(q_steps, [])
        return

    per_head = MLA_NOPE + MLA_V

    def kv_latent():
        h = (xn * kvn_ref[...]).astype(BF16)
        cell["c_kv"] = _rms_rows(_dot(h, w_dc_ref[...]), lnorm_ref[...]).astype(BF16)
        kr = _dot_nt(w_drt_ref[...], h)
        cell["ss_rope"] = jnp.sum(kr * kr, axis=0, keepdims=True)
        krg = kr * kg_ref[MLA_NOPE:, :]
        x1, x2 = krg[:half, :], krg[half:, :]
        cell["r1"] = x1 * cos - x2 * sin
        cell["r2"] = x1 * sin + x2 * cos

    def kv_heads(h0):
        def run():
            r0, r1 = h0 * per_head, (h0 + group) * per_head
            kvt_ref[r0:r1, :] = _dot_nt(w_upt_ref[r0:r1, :], cell["c_kv"])
            g_nope = kg_ref[:MLA_NOPE, :]
            for hh in range(h0, h0 + group):
                kn = kvt_ref[hh * per_head:hh * per_head + MLA_NOPE, :]
                ss = jnp.sum(kn * kn, axis=0, keepdims=True) + cell["ss_rope"]
                inv = lax.rsqrt(ss * (1.0 / MLA_QK) + EPS)
                kt = jnp.concatenate([kn * inv * g_nope, cell["r1"] * inv, cell["r2"] * inv, pad],
                                     axis=0)
                k_ref[hh] = kt.T.astype(BF16)
                vt_ref[hh, 0] = kvt_ref[hh * per_head + MLA_NOPE:(hh + 1) * per_head, :].astype(BF16)
        return run

    kv_steps = [kv_latent] + [kv_heads(h0) for h0 in range(0, MLA_HEADS, group)]
    _interleave(kv_steps, q_steps)


def _mla_qkv(x, pos, inv_freq, q_params, kv_params=None):
    s, d = x.shape
    rows = PROJ_ROWS
    q_pre, w_dq, q_lat, w_uq, q_gain = q_params
    w_uqt = w_uq.T.astype(BF16)
    row_spec = pl.BlockSpec((rows, d), lambda i: (i, 0))
    in_specs = [row_spec, pl.BlockSpec((1, rows), lambda i: (0, i)), _full((MLA_ROPE // 2, 1)),
                _full((1, d)), _full(w_dq.shape), _full((1, w_dq.shape[1])), _full(w_uqt.shape),
                _full((MLA_QK, 1))]
    args = [x, pos, inv_freq, q_pre.reshape(1, d), w_dq.astype(BF16), q_lat.reshape(1, -1), w_uqt,
            q_gain.reshape(-1, 1)]
    out_shape = [jax.ShapeDtypeStruct((MLA_HEADS, QK_PAD, s), BF16)]
    out_specs = [pl.BlockSpec((MLA_HEADS, QK_PAD, rows), lambda i: (0, 0, i))]
    scratch = [pltpu.VMEM((w_uqt.shape[0], rows), F32)]
    if kv_params is not None:
        kv_pre, w_down, kv_lat, w_up, k_gain = kv_params
        w_dc = w_down[:, :MLA_KV_RANK].astype(BF16)
        w_drt = w_down[:, MLA_KV_RANK:].T.astype(BF16)
        w_upt = w_up.T.astype(BF16)
        in_specs += [_full((1, d)), _full(w_dc.shape), _full(w_drt.shape), _full((1, MLA_KV_RANK)),
                     _full(w_upt.shape), _full((MLA_QK, 1))]
        args += [kv_pre.reshape(1, d), w_dc, w_drt, kv_lat.reshape(1, -1), w_upt,
                 k_gain.reshape(-1, 1)]
        out_shape += [jax.ShapeDtypeStruct((MLA_HEADS, s, QK_PAD), BF16),
                      jax.ShapeDtypeStruct((MLA_HEADS, s // rows, MLA_V, rows), BF16)]
        out_specs += [pl.BlockSpec((MLA_HEADS, rows, QK_PAD), lambda i: (0, i, 0)),
                      pl.BlockSpec((MLA_HEADS, 1, MLA_V, rows), lambda i: (0, i, 0, 0))]
        scratch += [pltpu.VMEM((w_upt.shape[0], rows), F32)]
    kern = functools.partial(_qkv_kernel, rows=rows, with_kv=kv_params is not None)
    return pl.pallas_call(
        kern,
        out_shape=tuple(out_shape),
        grid=(s // rows,),
        in_specs=in_specs,
        out_specs=tuple(out_specs),
        scratch_shapes=scratch,
        compiler_params=_params(("parallel",)),
        name="mla_qkv" if kv_params is not None else "mla_q",
    )(*args)


def _attn_kernel(bounded_ref, qt_ref, k_ref, vt_ref, o_ref, acc_ref, l_ref, p0_ref, p1_ref, *,
                 tq, tk):
    i = pl.program_id(1)
    per_q = tq // tk
    assert per_q == 2
    key_c = lax.broadcasted_iota(jnp.int32, (tk, tk), 0) >> 6
    qry_c = lax.broadcasted_iota(jnp.int32, (tk, tk), 1) >> 6
    diag_mask = key_c <= qry_c

    def k_tile(j):
        return k_ref[0, pl.ds(pl.multiple_of(j * tk, tk), tk), :]

    @pl.when(bounded_ref[0] != 0)
    def _():
        qt = qt_ref[0]

        def sublane_sums(p):
            return p.reshape(p.shape[0] // SUBLANES, SUBLANES, p.shape[1]).sum(axis=0)

        def weighted_values(p_ref, j):
            return _dot(vt_ref[0, j], p_ref[...])

        j0, j1 = 2 * i, 2 * i + 1
        p = jnp.where(diag_mask, jnp.exp2(_dot(k_tile(j1), qt[:, tk:])), 0.0)
        p1_ref[...] = jnp.concatenate([jnp.zeros((tk, tk), F32), p], axis=1).astype(BF16)
        l_ref[...] = jnp.concatenate([jnp.zeros((SUBLANES, tk), F32), sublane_sums(p)], axis=1)
        acc_ref[...] = jnp.zeros_like(acc_ref)

        def tile_pairs(t, n_pairs):
            half = tk // 2
            tot = l_tot = None
            for u in range(n_pairs):
                prev = 2 * (t + u) - 1
                if u == 0:
                    prev = jnp.where(t == 0, j1, prev)
                stages = ((p0_ref, 2 * (t + u), p1_ref, prev),
                          (p1_ref, 2 * (t + u) + 1, p0_ref, 2 * (t + u)))
                for new_ref, j_new, old_ref, j_old in stages:
                    for r0 in (0, half):
                        k_half = k_ref[0, pl.ds(pl.multiple_of(j_new * tk + r0, half), half), :]
                        p_new = jnp.exp2(_dot(k_half, qt))
                        l_part = sublane_sums(p_new)
                        part = _dot(vt_ref[0, j_old][:, r0:r0 + half], old_ref[r0:r0 + half, :])
                        tot = part if tot is None else tot + part
                        l_tot = l_part if l_tot is None else l_tot + l_part
                        new_ref[r0:r0 + half, :] = p_new.astype(BF16)
            acc_ref[...] += tot
            l_ref[...] += l_tot

        done = 0
        for n_pairs, trips in ((4, i >> 2), (2, (i >> 1) & 1), (1, i & 1)):
            def body(u, carry, n_pairs=n_pairs, done=done):
                tile_pairs(done + n_pairs * u, n_pairs)
                return carry

            lax.fori_loop(0, trips, body, 0)
            done = done + n_pairs * trips

        p = jnp.exp2(_dot(k_tile(j0), qt))
        p = jnp.concatenate([jnp.where(diag_mask, p[:, :tk], 0.0), p[:, tk:]], axis=1)
        acc = (acc_ref[...] + weighted_values(p1_ref, jnp.where(i == 0, j1, 2 * i - 1))
               + _dot(vt_ref[0, j0], p.astype(BF16)))
        l = jnp.sum(l_ref[...] + sublane_sums(p), axis=0, keepdims=True)
        o_ref[...] = (acc / l).astype(BF16)

    @pl.when(bounded_ref[0] == 0)
    def _():
        qt = qt_ref[0]

        def step(j, carry, mask):
            m, l, acc = carry
            s = _dot(k_tile(j), qt)
            if mask is not None:
                s = jnp.where(mask, s, -jnp.inf)
            m_new = jnp.maximum(m, jnp.max(s, axis=0, keepdims=True))
            alpha = jnp.exp2(m - m_new)
            p = jnp.exp2(s - m_new)
            l = alpha * l + jnp.sum(p, axis=0, keepdims=True)
            acc = alpha * acc + _dot(vt_ref[0, j], p.astype(BF16))
            return m_new, l, acc

        carry = (jnp.full((1, tq), -jnp.inf, F32), jnp.zeros((1, tq), F32),
                 jnp.zeros((MLA_V, tq), F32))
        carry = lax.fori_loop(0, i * per_q, lambda j, c: step(j, c, None), carry)
        key_w = lax.broadcasted_iota(jnp.int32, (tk, tq), 0) >> 6
        qry_w = lax.broadcasted_iota(jnp.int32, (tk, tq), 1) >> 6
        for jj in range(per_q):
            carry = step(i * per_q + jj, carry, (key_w + jj * (tk // CHUNK)) <= qry_w)
        _, l, acc = carry
        o_ref[...] = (acc / l).astype(BF16)


def _mla_attention(bounded, qt, k, vt):
    heads, _, s = qt.shape
    tq, tk = ATTN_TQ, ATTN_TK
    kern = functools.partial(_attn_kernel, tq=tq, tk=tk)
    return pl.pallas_call(
        kern,
        out_shape=jax.ShapeDtypeStruct((heads * MLA_V, s), BF16),
        grid_spec=pltpu.PrefetchScalarGridSpec(
            num_scalar_prefetch=1,
            grid=(heads, s // tq),
            in_specs=[
                pl.BlockSpec((1, QK_PAD, tq), lambda h, i, b: (h, 0, i)),
                pl.BlockSpec((1, s, QK_PAD), lambda h, i, b: (h, 0, 0)),
                pl.BlockSpec((1, s // tk, MLA_V, tk), lambda h, i, b: (h, 0, 0, 0)),
            ],
            out_specs=pl.BlockSpec((MLA_V, tq), lambda h, i, b: (h, i)),
            scratch_shapes=[pltpu.VMEM((MLA_V, tq), F32), pltpu.VMEM((SUBLANES, tq), F32),
                            pltpu.VMEM((tk, tq), BF16), pltpu.VMEM((tk, tq), BF16)],
        ),
        compiler_params=_params(("parallel", "arbitrary")),
        name="mla_attn",
    )(bounded, qt, k, vt)


def kernel(x, positions, a_norm, a_w_in, a_w_gate_up, a_b_gate, a_out_norm, a_w_out, b_norm, b_w_dq,
           b_q_latent_norm, b_w_uq, b_q_norm, b_w_out, kv_norm, kv_w_down, kv_latent_norm, kv_w_up,
           k_norm, f_norm, f_w_in, f_w_out):
    batch, s, d = x.shape
    n_a = a_norm.shape[0]
    n_b = b_norm.shape[0]
    half = MLA_ROPE // 2
    inv_freq = (ROPE_THETA ** (-jnp.arange(half, dtype=F32) / half)).reshape(half, 1)
    kv_params = (kv_norm, kv_w_down, kv_latent_norm, kv_w_up, k_norm)

    def q_params(j):
        return b_norm[j], b_w_dq[j], b_q_latent_norm[j], b_w_uq[j], b_q_norm[j]

    outs = []
    for bi in range(batch):
        xb = x[bi]
        pos = positions[bi].reshape(1, s)
        qt_first = k_sh = vt_sh = None
        for layer in range(n_a + n_b):
            if layer < n_a:
                xb = _gla_layer(xb, a_norm[layer], a_w_in[layer], a_w_gate_up[layer],
                                a_b_gate[layer], a_out_norm[layer], a_w_out[layer])
                xb = _ffn(xb, f_norm[layer], f_w_in[layer], f_w_out[layer])
            else:
                j = layer - n_a
                qt = qt_first if j == 0 else _mla_qkv(xb, pos, inv_freq, q_params(j))[0]
                logit_bound = (LOGIT_BOUND_COEF * jnp.max(jnp.abs(b_q_norm[j]))
                               * jnp.max(jnp.abs(k_norm)))
                bounded = (logit_bound <= MAX_BOUNDED_LOGIT).astype(jnp.int32).reshape(1)
                attn_t = _mla_attention(bounded, qt, k_sh, vt_sh)
                xb = _ffn(xb, f_norm[layer], f_w_in[layer], f_w_out[layer],
                          attn_t=attn_t, w_o=b_w_out[j])
            if layer == n_a - 1 and n_b > 0:
                qt_first, k_sh, vt_sh = _mla_qkv(xb, pos, inv_freq, q_params(0), kv_params)
        outs.append(xb)
    return jnp.stack(outs, axis=0)
```

```python
import functools
import itertools

import jax
import jax.numpy as jnp
from jax import lax
from jax.experimental import pallas as pl
from jax.experimental.pallas import tpu as pltpu

F32 = jnp.float32
BF16 = jnp.bfloat16

EPS = 1e-6
CHUNK = 64
GLA_HEADS = 4
GLA_TAU = 16.0
GLA_GATE_RANK = 16
MLA_HEADS = 16
MLA_NOPE = 64
MLA_ROPE = 32
MLA_QK = MLA_NOPE + MLA_ROPE
MLA_V = 64
MLA_KV_RANK = 256
ROPE_THETA = 10000.0
LOG2_E = 1.4426950408889634

MXU_COLS = 256
LANES = 128
SUBLANES = 8
QK_PAD = LANES
LOGIT_BOUND_COEF = 1.02 * LOG2_E * MLA_QK ** 0.5
MAX_BOUNDED_LOGIT = 100.0
V7X_VMEM_LIMIT_BYTES = 56 * 1024 * 1024

GLA_SUB_ROWS = 256
GLA_ROWS = 2 * GLA_SUB_ROWS
FFN_ROWS = 512
ATTN_TQ = 1024
ATTN_TK = ATTN_TQ // 2
ATTN_HEADS_PER_STEP = 2
PROJ_ROWS = ATTN_TK


def _dot(a, b):
    return jnp.dot(a, b, preferred_element_type=F32)


def _dot_nt(a, b):
    return lax.dot_general(a, b, (((1,), (1,)), ((), ())), preferred_element_type=F32)


def _dot_tn(a, b):
    return lax.dot_general(a, b, (((0,), (0,)), ((), ())), preferred_element_type=F32)


def _rms_rows(x, g):
    return x * lax.rsqrt(jnp.mean(x * x, axis=-1, keepdims=True) + EPS) * g


def _silu(x):
    return x * jax.nn.sigmoid(x)


def _params(semantics):
    return pltpu.CompilerParams(dimension_semantics=semantics,
                                vmem_limit_bytes=V7X_VMEM_LIMIT_BYTES)


def _interleave(first, second):
    for f, g in itertools.zip_longest(first, second):
        for step in (f, g):
            if step is not None:
                step()


def _full(shape):
    return pl.BlockSpec(shape, lambda *_: (0,) * len(shape), pipeline_mode=pl.Buffered(1))


def _gla_kernel(x_ref, xn_ref, norm_ref, w_main_ref, w_gl_ref, w_gate_ref, b_gate_ref, onorm_ref,
                w_out_ref, out_ref, state_ref, *scratch, sub, dk, dv):
    hk, hv = dk // GLA_HEADS, dv // GLA_HEADS
    subs = [scratch[0:3], scratch[3:6]]

    @pl.when(pl.program_id(0) == 0)
    def _():
        state_ref[...] = jnp.zeros_like(state_ref)

    r_i = lax.broadcasted_iota(jnp.int32, (sub, sub), 0)
    c_i = lax.broadcasted_iota(jnp.int32, (sub, sub), 1)
    tri = ((c_i <= r_i) & ((r_i >> 6) == (c_i >> 6))).astype(BF16)
    qi = lax.broadcasted_iota(jnp.int32, (CHUNK, CHUNK), 0)
    kj = lax.broadcasted_iota(jnp.int32, (CHUNK, CHUNK), 1)
    causal = kj <= qi
    q_scale = hk ** -0.5

    def project_pieces(x_rows, a):
        proj_ref, b_ref, _ = subs[a]
        cell = {}

        def norm():
            cell["h"] = _rms_rows(x_rows[...], norm_ref[...]).astype(BF16)

        def main_block(c0):
            def run():
                proj_ref[:, c0:c0 + MXU_COLS] = _dot(cell["h"], w_main_ref[:, c0:c0 + MXU_COLS])
            return run

        def gate_logits():
            gl = _dot(cell["h"], w_gl_ref[...]).astype(BF16)
            z = _dot(gl, w_gate_ref[...]) + b_gate_ref[...]
            cell["log_a"] = ((jnp.minimum(z, 0.0) - jnp.log1p(jnp.exp(-jnp.abs(z))))
                             * (1.0 / GLA_TAU))

        def cumulate():
            log_a = cell["log_a"]
            hi = log_a.astype(BF16)
            rest = log_a - hi.astype(F32)
            mid = rest.astype(BF16)
            lo = (rest - mid.astype(F32)).astype(BF16)
            b_ref[...] = _dot(tri, hi) + _dot(tri, mid) + _dot(tri, lo)

        blocks = [main_block(c0) for c0 in range(0, 2 * dk + 2 * dv, MXU_COLS)]
        return [norm, gate_logits] + blocks[:2] + [cumulate] + blocks[2:]

    def recur_pieces(a):
        proj_ref, b_ref, o_ref = subs[a]
        cell = {}

        def prepare(c):
            r0 = c * CHUNK
            b_c = b_ref[r0:r0 + CHUNK, :]
            b_last = b_c[CHUNK - 1:CHUNK, :]
            q_c = proj_ref[r0:r0 + CHUNK, 0:dk]
            k_c = proj_ref[r0:r0 + CHUNK, dk:2 * dk]
            cell[c] = dict(
                v=proj_ref[r0:r0 + CHUNK, 2 * dk:2 * dk + dv].astype(BF16),
                q_dec=(q_c * q_scale * jnp.exp(b_c)).astype(BF16),
                k_inv=(k_c * jnp.exp(-b_c)).astype(BF16),
                k_end=(k_c * jnp.exp(b_last - b_c)).astype(BF16),
                decay=jnp.exp(b_last))

        def unit(c, hh):
            def run():
                if hh == 0:
                    prepare(c)
                t = cell[c]
                r0 = c * CHUNK
                ks = slice(hh * hk, (hh + 1) * hk)
                vs = slice(hh * hv, (hh + 1) * hv)
                attn = jnp.where(causal, _dot_nt(t["q_dec"][:, ks], t["k_inv"][:, ks]),
                                 0.0).astype(BF16)
                state_t = state_ref[hh]
                o_h = (_dot(attn, t["v"][:, vs])
                       + _dot_nt(t["q_dec"][:, ks], state_t.astype(BF16)))
                state_ref[hh] = (t["decay"][:, ks] * state_t
                                 + _dot_tn(t["v"][:, vs], t["k_end"][:, ks]))
                o_ref[r0:r0 + CHUNK, vs] = _rms_rows(o_h, onorm_ref[...])
            return run

        return [unit(c, hh) for c in range(sub // CHUNK) for hh in range(GLA_HEADS)]

    def output_pieces(a):
        proj_ref, _, o_ref = subs[a]
        rows = slice(a * sub, (a + 1) * sub)
        cell = {}

        def gate():
            g = _silu(proj_ref[:, 2 * dk + dv:2 * dk + 2 * dv])
            cell["g"] = (o_ref[...] * g).astype(BF16)

        def out_block(c0):
            def run():
                cols = slice(c0, c0 + MXU_COLS)
                out_ref[rows, cols] = x_ref[rows, cols] + _dot(cell["g"], w_out_ref[:, cols])
            return run

        return [gate] + [out_block(c0) for c0 in range(0, out_ref.shape[1], MXU_COLS)]

    @pl.when(pl.program_id(0) == 0)
    def _():
        _interleave(project_pieces(x_ref.at[0:sub, :], 0), [])

    next_proj = project_pieces(xn_ref, 0)
    out0, out1 = output_pieces(0), output_pieces(1)
    split = len(next_proj) - len(out1) - 1
    _interleave(recur_pieces(0), project_pieces(x_ref.at[sub:2 * sub, :], 1))
    _interleave(recur_pieces(1), out0 + next_proj[:split])
    _interleave(out1, next_proj[split:])


def _gla_layer(x, norm, w_in, w_gate_up, b_gate, out_norm, w_out):
    s, d = x.shape
    dk = w_gate_up.shape[1]
    dv = (w_in.shape[1] - 2 * dk - GLA_GATE_RANK) // 2
    n_main = 2 * dk + 2 * dv
    rows = GLA_ROWS
    w_main = w_in[:, :n_main].astype(BF16)
    w_gl = jnp.pad(w_in[:, n_main:], ((0, 0), (0, LANES - GLA_GATE_RANK))).astype(BF16)
    w_gate = jnp.pad(w_gate_up, ((0, LANES - GLA_GATE_RANK), (0, 0))).astype(BF16)
    sub = GLA_SUB_ROWS
    last_sub = s // sub - 1
    sub_scratch = [pltpu.VMEM((sub, n_main), F32), pltpu.VMEM((sub, dk), F32),
                   pltpu.VMEM((sub, dv), F32)]
    kern = functools.partial(_gla_kernel, sub=sub, dk=dk, dv=dv)
    return pl.pallas_call(
        kern,
        out_shape=jax.ShapeDtypeStruct((s, d), F32),
        grid=(s // rows,),
        in_specs=[
            pl.BlockSpec((rows, d), lambda i: (i, 0)),
            pl.BlockSpec((sub, d), lambda i: (jnp.minimum(2 * i + 2, last_sub), 0)),
            _full((1, d)),
            _full((d, n_main)),
            _full((d, LANES)),
            _full((LANES, dk)),
            _full((1, dk)),
            _full((1, dv // GLA_HEADS)),
            _full((dv, d)),
        ],
        out_specs=pl.BlockSpec((rows, d), lambda i: (i, 0)),
        scratch_shapes=[pltpu.VMEM((GLA_HEADS, dv // GLA_HEADS, dk // GLA_HEADS), F32)]
        + sub_scratch * 2,
        compiler_params=_params(("arbitrary",)),
        name="gla_layer",
    )(x, x, norm.reshape(1, d), w_main, w_gl, w_gate, b_gate.reshape(1, dk),
      out_norm.reshape(1, -1), w_out.astype(BF16))


def _ffn_body(x, norm_ref, w_in_ref, w_out_ref, out_ref, hidden):
    h = _rms_rows(x, norm_ref[...]).astype(BF16)
    gate = _dot(h, w_in_ref[:, 0:hidden])
    up = _dot(h, w_in_ref[:, hidden:2 * hidden])
    act = (_silu(gate) * up).astype(BF16)
    out_ref[...] = x + _dot(act, w_out_ref[...])


def _ffn_kernel(x_ref, norm_ref, w_in_ref, w_out_ref, out_ref, *, hidden):
    _ffn_body(x_ref[...], norm_ref, w_in_ref, w_out_ref, out_ref, hidden)


def _proj_ffn_kernel(x_ref, ot_ref, w_o_ref, norm_ref, w_in_ref, w_out_ref, out_ref, *, hidden):
    x = x_ref[...] + _dot_tn(ot_ref[...], w_o_ref[...])
    _ffn_body(x, norm_ref, w_in_ref, w_out_ref, out_ref, hidden)


def _ffn(x, norm, w_in, w_out, attn_t=None, w_o=None):
    s, d = x.shape
    hidden = w_out.shape[0]
    rows = FFN_ROWS
    row_spec = pl.BlockSpec((rows, d), lambda i: (i, 0))
    ffn_specs = [_full((1, d)), _full((d, 2 * hidden)), _full((hidden, d))]
    ffn_args = (norm.reshape(1, d), w_in.astype(BF16), w_out.astype(BF16))
    if attn_t is None:
        kern = functools.partial(_ffn_kernel, hidden=hidden)
        in_specs = [row_spec] + ffn_specs
        args = (x,) + ffn_args
        name = "ffn"
    else:
        kern = functools.partial(_proj_ffn_kernel, hidden=hidden)
        in_specs = [row_spec, pl.BlockSpec((attn_t.shape[0], rows), lambda i: (0, i)),
                    _full(w_o.shape)] + ffn_specs
        args = (x, attn_t, w_o.astype(BF16)) + ffn_args
        name = "attn_proj_ffn"
    return pl.pallas_call(
        kern,
        out_shape=jax.ShapeDtypeStruct((s, d), F32),
        grid=(s // rows,),
        in_specs=in_specs,
        out_specs=row_spec,
        compiler_params=_params(("parallel",)),
        name=name,
    )(*args)


def _qkv_kernel(*refs, rows, with_kv):
    if with_kv:
        (x_ref, pos_ref, inv_freq_ref, qn_ref, w_dq_ref, qln_ref, w_uqt_ref, qg_ref,
         kvn_ref, w_dc_ref, w_drt_ref, lnorm_ref, w_upt_ref, kg_ref,
         qt_ref, k_ref, vt_ref, qall_ref, kvt_ref) = refs
    else:
        (x_ref, pos_ref, inv_freq_ref, qn_ref, w_dq_ref, qln_ref, w_uqt_ref, qg_ref,
         qt_ref, qall_ref) = refs
    half = MLA_ROPE // 2
    group = 4
    x = x_ref[...]
    xn = x * lax.rsqrt(jnp.mean(x * x, axis=-1, keepdims=True) + EPS)
    ang = pos_ref[...].astype(F32) * inv_freq_ref[...]
    cos, sin = jnp.cos(ang), jnp.sin(ang)
    pad = jnp.zeros((QK_PAD - MLA_QK, rows), F32)
    cell = {}

    def q_latent():
        h = (xn * qn_ref[...]).astype(BF16)
        cell["c_q"] = _rms_rows(_dot(h, w_dq_ref[...]), qln_ref[...]).astype(BF16)

    def q_heads(h0):
        def run():
            r0, r1 = h0 * MLA_QK, (h0 + group) * MLA_QK
            qall_ref[r0:r1, :] = _dot_nt(w_uqt_ref[r0:r1, :], cell["c_q"])
            g = qg_ref[...]
            scale = MLA_QK ** -0.5 * LOG2_E
            for hh in range(h0, h0 + group):
                q = qall_ref[hh * MLA_QK:(hh + 1) * MLA_QK, :]
                inv = lax.rsqrt(jnp.mean(q * q, axis=0, keepdims=True) + EPS)
                qn = q * inv * g
                x1, x2 = qn[MLA_NOPE:MLA_NOPE + half, :], qn[MLA_NOPE + half:, :]
                qt = jnp.concatenate([qn[:MLA_NOPE, :], x1 * cos - x2 * sin,
                                      x1 * sin + x2 * cos, pad], axis=0)
                qt_ref[hh] = (qt * scale).astype(BF16)
        return run

    q_steps = [q_latent] + [q_heads(h0) for h0 in range(0, MLA_HEADS, group)]
    if not with_kv:
        _interleave(q_steps, [])
        return

    per_head = MLA_NOPE + MLA_V

    def kv_latent():
        h = (xn * kvn_ref[...]).astype(BF16)
        cell["c_kv"] = _rms_rows(_dot(h, w_dc_ref[...]), lnorm_ref[...]).astype(BF16)
        kr = _dot_nt(w_drt_ref[...], h)
        cell["ss_rope"] = jnp.sum(kr * kr, axis=0, keepdims=True)
        krg = kr * kg_ref[MLA_NOPE:, :]
        x1, x2 = krg[:half, :], krg[half:, :]
        cell["r1"] = x1 * cos - x2 * sin
        cell["r2"] = x1 * sin + x2 * cos

    def kv_heads(h0):
        def run():
            r0, r1 = h0 * per_head, (h0 + group) * per_head
            kvt_ref[r0:r1, :] = _dot_nt(w_upt_ref[r0:r1, :], cell["c_kv"])
            g_nope = kg_ref[:MLA_NOPE, :]
            for hh in range(h0, h0 + group):
                kn = kvt_ref[hh * per_head:hh * per_head + MLA_NOPE, :]
                ss = jnp.sum(kn * kn, axis=0, keepdims=True) + cell["ss_rope"]
                inv = lax.rsqrt(ss * (1.0 / MLA_QK) + EPS)
                kt = jnp.concatenate([kn * inv * g_nope, cell["r1"] * inv, cell["r2"] * inv, pad],
                                     axis=0)
                k_ref[hh] = kt.T.astype(BF16)
                vt_ref[hh, 0] = kvt_ref[hh * per_head + MLA_NOPE:(hh + 1) * per_head, :].astype(BF16)
        return run

    kv_steps = [kv_latent] + [kv_heads(h0) for h0 in range(0, MLA_HEADS, group)]
    _interleave(kv_steps, q_steps)


def _mla_qkv(x, pos, inv_freq, q_params, kv_params=None):
    s, d = x.shape
    rows = PROJ_ROWS
    q_pre, w_dq, q_lat, w_uq, q_gain = q_params
    w_uqt = w_uq.T.astype(BF16)
    row_spec = pl.BlockSpec((rows, d), lambda i: (i, 0))
    in_specs = [row_spec, pl.BlockSpec((1, rows), lambda i: (0, i)), _full((MLA_ROPE // 2, 1)),
                _full((1, d)), _full(w_dq.shape), _full((1, w_dq.shape[1])), _full(w_uqt.shape),
                _full((MLA_QK, 1))]
    args = [x, pos, inv_freq, q_pre.reshape(1, d), w_dq.astype(BF16), q_lat.reshape(1, -1), w_uqt,
            q_gain.reshape(-1, 1)]
    out_shape = [jax.ShapeDtypeStruct((MLA_HEADS, QK_PAD, s), BF16)]
    out_specs = [pl.BlockSpec((MLA_HEADS, QK_PAD, rows), lambda i: (0, 0, i))]
    scratch = [pltpu.VMEM((w_uqt.shape[0], rows), F32)]
    if kv_params is not None:
        kv_pre, w_down, kv_lat, w_up, k_gain = kv_params
        w_dc = w_down[:, :MLA_KV_RANK].astype(BF16)
        w_drt = w_down[:, MLA_KV_RANK:].T.astype(BF16)
        w_upt = w_up.T.astype(BF16)
        in_specs += [_full((1, d)), _full(w_dc.shape), _full(w_drt.shape), _full((1, MLA_KV_RANK)),
                     _full(w_upt.shape), _full((MLA_QK, 1))]
        args += [kv_pre.reshape(1, d), w_dc, w_drt, kv_lat.reshape(1, -1), w_upt,
                 k_gain.reshape(-1, 1)]
        out_shape += [jax.ShapeDtypeStruct((MLA_HEADS, s, QK_PAD), BF16),
                      jax.ShapeDtypeStruct((MLA_HEADS, s // rows, MLA_V, rows), BF16)]
        out_specs += [pl.BlockSpec((MLA_HEADS, rows, QK_PAD), lambda i: (0, i, 0)),
                      pl.BlockSpec((MLA_HEADS, 1, MLA_V, rows), lambda i: (0, i, 0, 0))]
        scratch += [pltpu.VMEM((w_upt.shape[0], rows), F32)]
    kern = functools.partial(_qkv_kernel, rows=rows, with_kv=kv_params is not None)
    return pl.pallas_call(
        kern,
        out_shape=tuple(out_shape),
        grid=(s // rows,),
        in_specs=in_specs,
        out_specs=tuple(out_specs),
        scratch_shapes=scratch,
        compiler_params=_params(("parallel",)),
        name="mla_qkv" if kv_params is not None else "mla_q",
    )(*args)


def _attn_kernel(bounded_ref, qt_ref, k_ref, vt_ref, o_ref, acc_ref, l_ref, p0_ref, p1_ref, *,
                 tq, tk):
    i = pl.program_id(1)
    heads = qt_ref.shape[0]
    per_q = tq // tk
    assert per_q == 2
    key_c = lax.broadcasted_iota(jnp.int32, (tk, tk), 0) >> 6
    qry_c = lax.broadcasted_iota(jnp.int32, (tk, tk), 1) >> 6
    diag_mask = key_c <= qry_c

    def k_rows(hd, start, size):
        return k_ref[hd, pl.ds(pl.multiple_of(start, size), size), :]

    @pl.when(bounded_ref[0] != 0)
    def _():
        qts = [qt_ref[hd] for hd in range(heads)]

        def sublane_sums(p):
            return p.reshape(p.shape[0] // SUBLANES, SUBLANES, p.shape[1]).sum(axis=0)

        def weighted_values(hd, p_ref, j):
            return _dot(vt_ref[hd, j], p_ref[hd])

        j0, j1 = 2 * i, 2 * i + 1
        for hd in range(heads):
            p = jnp.where(diag_mask, jnp.exp2(_dot(k_rows(hd, j1 * tk, tk), qts[hd][:, tk:])), 0.0)
            p1_ref[hd] = jnp.concatenate([jnp.zeros((tk, tk), F32), p], axis=1).astype(BF16)
            l_ref[hd] = jnp.concatenate([jnp.zeros((SUBLANES, tk), F32), sublane_sums(p)], axis=1)
            acc_ref[hd] = jnp.zeros((MLA_V, tq), F32)

        def tile_pairs(t, n_pairs):
            half = tk // 2
            tot = [None] * heads
            l_tot = [None] * heads
            for u in range(n_pairs):
                prev = 2 * (t + u) - 1
                if u == 0:
                    prev = jnp.where(t == 0, j1, prev)
                stages = ((p0_ref, 2 * (t + u), p1_ref, prev),
                          (p1_ref, 2 * (t + u) + 1, p0_ref, 2 * (t + u)))
                for new_ref, j_new, old_ref, j_old in stages:
                    for r0 in (0, half):
                        for hd in range(heads):
                            k_half = k_rows(hd, j_new * tk + r0, half)
                            p_new = jnp.exp2(_dot(k_half, qts[hd]))
                            l_part = sublane_sums(p_new)
                            part = _dot(vt_ref[hd, j_old][:, r0:r0 + half],
                                        old_ref[hd, r0:r0 + half, :])
                            tot[hd] = part if tot[hd] is None else tot[hd] + part
                            l_tot[hd] = l_part if l_tot[hd] is None else l_tot[hd] + l_part
                            new_ref[hd, r0:r0 + half, :] = p_new.astype(BF16)
            for hd in range(heads):
                acc_ref[hd] += tot[hd]
                l_ref[hd] += l_tot[hd]

        done = 0
        for n_pairs, trips in ((4, i >> 2), (2, (i >> 1) & 1), (1, i & 1)):
            def body(u, carry, n_pairs=n_pairs, done=done):
                tile_pairs(done + n_pairs * u, n_pairs)
                return carry

            lax.fori_loop(0, trips, body, 0)
            done = done + n_pairs * trips

        j_last = jnp.where(i == 0, j1, 2 * i - 1)
        for hd in range(heads):
            p = jnp.exp2(_dot(k_rows(hd, j0 * tk, tk), qts[hd]))
            p = jnp.concatenate([jnp.where(diag_mask, p[:, :tk], 0.0), p[:, tk:]], axis=1)
            acc = (acc_ref[hd] + weighted_values(hd, p1_ref, j_last)
                   + _dot(vt_ref[hd, j0], p.astype(BF16)))
            l = jnp.sum(l_ref[hd] + sublane_sums(p), axis=0, keepdims=True)
            o_ref[hd * MLA_V:(hd + 1) * MLA_V, :] = (acc / l).astype(BF16)

    @pl.when(bounded_ref[0] == 0)
    def _():
        key_w = lax.broadcasted_iota(jnp.int32, (tk, tq), 0) >> 6
        qry_w = lax.broadcasted_iota(jnp.int32, (tk, tq), 1) >> 6
        for hd in range(heads):
            qt = qt_ref[hd]

            def step(j, carry, mask, hd=hd, qt=qt):
                m, l, acc = carry
                s = _dot(k_rows(hd, j * tk, tk), qt)
                if mask is not None:
                    s = jnp.where(mask, s, -jnp.inf)
                m_new = jnp.maximum(m, jnp.max(s, axis=0, keepdims=True))
                alpha = jnp.exp2(m - m_new)
                p = jnp.exp2(s - m_new)
                l = alpha * l + jnp.sum(p, axis=0, keepdims=True)
                acc = alpha * acc + _dot(vt_ref[hd, j], p.astype(BF16))
                return m_new, l, acc

            carry = (jnp.full((1, tq), -jnp.inf, F32), jnp.zeros((1, tq), F32),
                     jnp.zeros((MLA_V, tq), F32))
            carry = lax.fori_loop(0, i * per_q, lambda j, c, step=step: step(j, c, None), carry)
            for jj in range(per_q):
                carry = step(i * per_q + jj, carry, (key_w + jj * (tk // CHUNK)) <= qry_w)
            _, l, acc = carry
            o_ref[hd * MLA_V:(hd + 1) * MLA_V, :] = (acc / l).astype(BF16)


def _mla_attention(bounded, qt, k, vt):
    heads, _, s = qt.shape
    tq, tk, hb = ATTN_TQ, ATTN_TK, ATTN_HEADS_PER_STEP
    kern = functools.partial(_attn_kernel, tq=tq, tk=tk)
    return pl.pallas_call(
        kern,
        out_shape=jax.ShapeDtypeStruct((heads * MLA_V, s), BF16),
        grid_spec=pltpu.PrefetchScalarGridSpec(
            num_scalar_prefetch=1,
            grid=(heads // hb, s // tq),
            in_specs=[
                pl.BlockSpec((hb, QK_PAD, tq), lambda h, i, b: (h, 0, i)),
                pl.BlockSpec((hb, s, QK_PAD), lambda h, i, b: (h, 0, 0)),
                pl.BlockSpec((hb, s // tk, MLA_V, tk), lambda h, i, b: (h, 0, 0, 0)),
            ],
            out_specs=pl.BlockSpec((hb * MLA_V, tq), lambda h, i, b: (h, i)),
            scratch_shapes=[pltpu.VMEM((hb, MLA_V, tq), F32), pltpu.VMEM((hb, SUBLANES, tq), F32),
                            pltpu.VMEM((hb, tk, tq), BF16), pltpu.VMEM((hb, tk, tq), BF16)],
        ),
        compiler_params=_params(("parallel", "arbitrary")),
        name="mla_attn",
    )(bounded, qt, k, vt)


def kernel(x, positions, a_norm, a_w_in, a_w_gate_up, a_b_gate, a_out_norm, a_w_out, b_norm, b_w_dq,
           b_q_latent_norm, b_w_uq, b_q_norm, b_w_out, kv_norm, kv_w_down, kv_latent_norm, kv_w_up,
           k_norm, f_norm, f_w_in, f_w_out):
    batch, s, d = x.shape
    n_a = a_norm.shape[0]
    n_b = b_norm.shape[0]
    half = MLA_ROPE // 2
    inv_freq = (ROPE_THETA ** (-jnp.arange(half, dtype=F32) / half)).reshape(half, 1)
    kv_params = (kv_norm, kv_w_down, kv_latent_norm, kv_w_up, k_norm)

    def q_params(j):
        return b_norm[j], b_w_dq[j], b_q_latent_norm[j], b_w_uq[j], b_q_norm[j]

    outs = []
    for bi in range(batch):
        xb = x[bi]
        pos = positions[bi].reshape(1, s)
        qt_first = k_sh = vt_sh = None
        for layer in range(n_a + n_b):
            if layer < n_a:
                xb = _gla_layer(xb, a_norm[layer], a_w_in[layer], a_w_gate_up[layer],
                                a_b_gate[layer], a_out_norm[layer], a_w_out[layer])
                xb = _ffn(xb, f_norm[layer], f_w_in[layer], f_w_out[layer])
            else:
                j = layer - n_a
                qt = qt_first if j == 0 else _mla_qkv(xb, pos, inv_freq, q_params(j))[0]
                logit_bound = (LOGIT_BOUND_COEF * jnp.max(jnp.abs(b_q_norm[j]))
                               * jnp.max(jnp.abs(k_norm)))
                bounded = (logit_bound <= MAX_BOUNDED_LOGIT).astype(jnp.int32).reshape(1)
                attn_t = _mla_attention(bounded, qt, k_sh, vt_sh)
                xb = _ffn(xb, f_norm[layer], f_w_in[layer], f_w_out[layer],
                          attn_t=attn_t, w_o=b_w_out[j])
            if layer == n_a - 1 and n_b > 0:
                qt_first, k_sh, vt_sh = _mla_qkv(xb, pos, inv_freq, q_params(0), kv_params)
        outs.append(xb)
    return jnp.stack(outs, axis=0)
```

```python
import functools
import itertools

import jax
import jax.numpy as jnp
from jax import lax
from jax.experimental import pallas as pl
from jax.experimental.pallas import tpu as pltpu

F32 = jnp.float32
BF16 = jnp.bfloat16

EPS = 1e-6
CHUNK = 64
GLA_HEADS = 4
GLA_TAU = 16.0
GLA_GATE_RANK = 16
MLA_HEADS = 16
MLA_NOPE = 64
MLA_ROPE = 32
MLA_QK = MLA_NOPE + MLA_ROPE
MLA_V = 64
MLA_KV_RANK = 256
ROPE_THETA = 10000.0
LOG2_E = 1.4426950408889634

MXU_COLS = 256
LANES = 128
SUBLANES = 8
QK_PAD = LANES
LOGIT_BOUND_COEF = 1.02 * LOG2_E * MLA_QK ** 0.5
MAX_BOUNDED_LOGIT = 100.0
V7X_VMEM_LIMIT_BYTES = 56 * 1024 * 1024

GLA_SUB_ROWS = 256
GLA_ROWS = 2 * GLA_SUB_ROWS
FFN_ROWS = 512
ATTN_TQ = 1024
ATTN_TK = ATTN_TQ // 2
ATTN_HEADS_PER_STEP = 2
PROJ_ROWS = ATTN_TK


def _dot(a, b):
    return jnp.dot(a, b, preferred_element_type=F32)


def _dot_nt(a, b):
    return lax.dot_general(a, b, (((1,), (1,)), ((), ())), preferred_element_type=F32)


def _dot_tn(a, b):
    return lax.dot_general(a, b, (((0,), (0,)), ((), ())), preferred_element_type=F32)


def _rms_rows(x, g):
    return x * lax.rsqrt(jnp.mean(x * x, axis=-1, keepdims=True) + EPS) * g


def _silu(x):
    return x * jax.nn.sigmoid(x)


def _params(semantics):
    return pltpu.CompilerParams(dimension_semantics=semantics,
                                vmem_limit_bytes=V7X_VMEM_LIMIT_BYTES)


def _interleave(first, second):
    for f, g in itertools.zip_longest(first, second):
        for step in (f, g):
            if step is not None:
                step()


def _full(shape):
    return pl.BlockSpec(shape, lambda *_: (0,) * len(shape), pipeline_mode=pl.Buffered(1))


def _gla_kernel(x_ref, xn_ref, norm_ref, w_main_ref, w_gl_ref, w_gate_ref, b_gate_ref, onorm_ref,
                w_out_ref, out_ref, state_ref, *scratch, sub, dk, dv):
    hk, hv = dk // GLA_HEADS, dv // GLA_HEADS
    subs = [scratch[0:3], scratch[3:6]]

    @pl.when(pl.program_id(0) == 0)
    def _():
        state_ref[...] = jnp.zeros_like(state_ref)

    r_i = lax.broadcasted_iota(jnp.int32, (sub, sub), 0)
    c_i = lax.broadcasted_iota(jnp.int32, (sub, sub), 1)
    tri = ((c_i <= r_i) & ((r_i >> 6) == (c_i >> 6))).astype(BF16)
    qi = lax.broadcasted_iota(jnp.int32, (CHUNK, CHUNK), 0)
    kj = lax.broadcasted_iota(jnp.int32, (CHUNK, CHUNK), 1)
    causal = kj <= qi
    q_scale = hk ** -0.5

    def project_pieces(x_rows, a):
        proj_ref, b_ref, _ = subs[a]
        cell = {}

        def norm():
            cell["h"] = _rms_rows(x_rows[...], norm_ref[...]).astype(BF16)

        def main_block(c0):
            def run():
                proj_ref[:, c0:c0 + MXU_COLS] = _dot(cell["h"], w_main_ref[:, c0:c0 + MXU_COLS])
            return run

        def gate_logits():
            gl = _dot(cell["h"], w_gl_ref[...]).astype(BF16)
            z = _dot(gl, w_gate_ref[...]) + b_gate_ref[...]
            cell["log_a"] = ((jnp.minimum(z, 0.0) - jnp.log1p(jnp.exp(-jnp.abs(z))))
                             * (1.0 / GLA_TAU))

        def cumulate():
            log_a = cell["log_a"]
            hi = log_a.astype(BF16)
            rest = log_a - hi.astype(F32)
            mid = rest.astype(BF16)
            lo = (rest - mid.astype(F32)).astype(BF16)
            b_ref[...] = _dot(tri, hi) + _dot(tri, mid) + _dot(tri, lo)

        blocks = [main_block(c0) for c0 in range(0, 2 * dk + 2 * dv, MXU_COLS)]
        return [norm, gate_logits] + blocks[:2] + [cumulate] + blocks[2:]

    def recur_pieces(a):
        proj_ref, b_ref, o_ref = subs[a]
        cell = {}

        def prepare(c):
            r0 = c * CHUNK
            b_c = b_ref[r0:r0 + CHUNK, :]
            b_last = b_c[CHUNK - 1:CHUNK, :]
            q_c = proj_ref[r0:r0 + CHUNK, 0:dk]
            k_c = proj_ref[r0:r0 + CHUNK, dk:2 * dk]
            cell[c] = dict(
                v=proj_ref[r0:r0 + CHUNK, 2 * dk:2 * dk + dv].astype(BF16),
                q_dec=(q_c * q_scale * jnp.exp(b_c)).astype(BF16),
                k_inv=(k_c * jnp.exp(-b_c)).astype(BF16),
                k_end=(k_c * jnp.exp(b_last - b_c)).astype(BF16),
                decay=jnp.exp(b_last))

        def unit(c, hh):
            def run():
                if hh == 0:
                    prepare(c)
                t = cell[c]
                r0 = c * CHUNK
                ks = slice(hh * hk, (hh + 1) * hk)
                vs = slice(hh * hv, (hh + 1) * hv)
                attn = jnp.where(causal, _dot_nt(t["q_dec"][:, ks], t["k_inv"][:, ks]),
                                 0.0).astype(BF16)
                state_t = state_ref[hh]
                o_h = (_dot(attn, t["v"][:, vs])
                       + _dot_nt(t["q_dec"][:, ks], state_t.astype(BF16)))
                state_ref[hh] = (t["decay"][:, ks] * state_t
                                 + _dot_tn(t["v"][:, vs], t["k_end"][:, ks]))
                o_ref[r0:r0 + CHUNK, vs] = _rms_rows(o_h, onorm_ref[...])
            return run

        return [unit(c, hh) for c in range(sub // CHUNK) for hh in range(GLA_HEADS)]

    def output_pieces(a):
        proj_ref, _, o_ref = subs[a]
        rows = slice(a * sub, (a + 1) * sub)
        cell = {}

        def gate():
            g = _silu(proj_ref[:, 2 * dk + dv:2 * dk + 2 * dv])
            cell["g"] = (o_ref[...] * g).astype(BF16)

        def out_block(c0):
            def run():
                cols = slice(c0, c0 + MXU_COLS)
                out_ref[rows, cols] = x_ref[rows, cols] + _dot(cell["g"], w_out_ref[:, cols])
            return run

        return [gate] + [out_block(c0) for c0 in range(0, out_ref.shape[1], MXU_COLS)]

    @pl.when(pl.program_id(0) == 0)
    def _():
        _interleave(project_pieces(x_ref.at[0:sub, :], 0), [])

    next_proj = project_pieces(xn_ref, 0)
    out0, out1 = output_pieces(0), output_pieces(1)
    split = len(next_proj) - len(out1) - 1
    _interleave(recur_pieces(0), project_pieces(x_ref.at[sub:2 * sub, :], 1))
    _interleave(recur_pieces(1), out0 + next_proj[:split])
    _interleave(out1, next_proj[split:])


def _gla_layer(x, norm, w_in, w_gate_up, b_gate, out_norm, w_out):
    s, d = x.shape
    dk = w_gate_up.shape[1]
    dv = (w_in.shape[1] - 2 * dk - GLA_GATE_RANK) // 2
    n_main = 2 * dk + 2 * dv
    rows = GLA_ROWS
    w_main = w_in[:, :n_main].astype(BF16)
    w_gl = jnp.pad(w_in[:, n_main:], ((0, 0), (0, LANES - GLA_GATE_RANK))).astype(BF16)
    w_gate = jnp.pad(w_gate_up, ((0, LANES - GLA_GATE_RANK), (0, 0))).astype(BF16)
    sub = GLA_SUB_ROWS
    last_sub = s // sub - 1
    sub_scratch = [pltpu.VMEM((sub, n_main), F32), pltpu.VMEM((sub, dk), F32),
                   pltpu.VMEM((sub, dv), F32)]
    kern = functools.partial(_gla_kernel, sub=sub, dk=dk, dv=dv)
    return pl.pallas_call(
        kern,
        out_shape=jax.ShapeDtypeStruct((s, d), F32),
        grid=(s // rows,),
        in_specs=[
            pl.BlockSpec((rows, d), lambda i: (i, 0)),
            pl.BlockSpec((sub, d), lambda i: (jnp.minimum(2 * i + 2, last_sub), 0)),
            _full((1, d)),
            _full((d, n_main)),
            _full((d, LANES)),
            _full((LANES, dk)),
            _full((1, dk)),
            _full((1, dv // GLA_HEADS)),
            _full((dv, d)),
        ],
        out_specs=pl.BlockSpec((rows, d), lambda i: (i, 0)),
        scratch_shapes=[pltpu.VMEM((GLA_HEADS, dv // GLA_HEADS, dk // GLA_HEADS), F32)]
        + sub_scratch * 2,
        compiler_params=_params(("arbitrary",)),
        name="gla_layer",
    )(x, x, norm.reshape(1, d), w_main, w_gl, w_gate, b_gate.reshape(1, dk),
      out_norm.reshape(1, -1), w_out.astype(BF16))


def _ffn_body(x, norm_ref, w_in_ref, w_out_ref, out_ref, hidden):
    h = _rms_rows(x, norm_ref[...]).astype(BF16)
    gate = _dot(h, w_in_ref[:, 0:hidden])
    up = _dot(h, w_in_ref[:, hidden:2 * hidden])
    act = (_silu(gate) * up).astype(BF16)
    out_ref[...] = x + _dot(act, w_out_ref[...])


def _ffn_kernel(x_ref, norm_ref, w_in_ref, w_out_ref, out_ref, *, hidden):
    _ffn_body(x_ref[...], norm_ref, w_in_ref, w_out_ref, out_ref, hidden)


def _proj_ffn_kernel(x_ref, ot_ref, w_o_ref, norm_ref, w_in_ref, w_out_ref, out_ref, *, hidden):
    x = x_ref[...] + _dot_tn(ot_ref[...], w_o_ref[...])
    _ffn_body(x, norm_ref, w_in_ref, w_out_ref, out_ref, hidden)


def _ffn(x, norm, w_in, w_out, attn_t=None, w_o=None):
    s, d = x.shape
    hidden = w_out.shape[0]
    rows = FFN_ROWS
    row_spec = pl.BlockSpec((rows, d), lambda i: (i, 0))
    ffn_specs = [_full((1, d)), _full((d, 2 * hidden)), _full((hidden, d))]
    ffn_args = (norm.reshape(1, d), w_in.astype(BF16), w_out.astype(BF16))
    if attn_t is None:
        kern = functools.partial(_ffn_kernel, hidden=hidden)
        in_specs = [row_spec] + ffn_specs
        args = (x,) + ffn_args
        name = "ffn"
    else:
        kern = functools.partial(_proj_ffn_kernel, hidden=hidden)
        in_specs = [row_spec, pl.BlockSpec((attn_t.shape[0], rows), lambda i: (0, i)),
                    _full(w_o.shape)] + ffn_specs
        args = (x, attn_t, w_o.astype(BF16)) + ffn_args
        name = "attn_proj_ffn"
    return pl.pallas_call(
        kern,
        out_shape=jax.ShapeDtypeStruct((s, d), F32),
        grid=(s // rows,),
        in_specs=in_specs,
        out_specs=row_spec,
        compiler_params=_params(("parallel",)),
        name=name,
    )(*args)


def _qkv_kernel(*refs, rows, with_kv):
    if with_kv:
        (x_ref, pos_ref, inv_freq_ref, qn_ref, w_dq_ref, qln_ref, w_uqt_ref, qg_ref,
         kvn_ref, w_dc_ref, w_drt_ref, lnorm_ref, w_upt_ref, kg_ref,
         qt_ref, k_ref, vt_ref, qall_ref, kvt_ref) = refs
    else:
        (x_ref, pos_ref, inv_freq_ref, qn_ref, w_dq_ref, qln_ref, w_uqt_ref, qg_ref,
         qt_ref, qall_ref) = refs
    half = MLA_ROPE // 2
    group = 4
    x = x_ref[...]
    xn = x * lax.rsqrt(jnp.mean(x * x, axis=-1, keepdims=True) + EPS)
    ang = pos_ref[...].astype(F32) * inv_freq_ref[...]
    cos, sin = jnp.cos(ang), jnp.sin(ang)
    pad = jnp.zeros((QK_PAD - MLA_QK, rows), F32)
    cell = {}

    def q_latent():
        h = (xn * qn_ref[...]).astype(BF16)
        cell["c_q"] = _rms_rows(_dot(h, w_dq_ref[...]), qln_ref[...]).astype(BF16)

    def q_heads(h0):
        def run():
            r0, r1 = h0 * MLA_QK, (h0 + group) * MLA_QK
            qall_ref[r0:r1, :] = _dot_nt(w_uqt_ref[r0:r1, :], cell["c_q"])
            g = qg_ref[...]
            scale = MLA_QK ** -0.5 * LOG2_E
            for hh in range(h0, h0 + group):
                q = qall_ref[hh * MLA_QK:(hh + 1) * MLA_QK, :]
                inv = lax.rsqrt(jnp.mean(q * q, axis=0, keepdims=True) + EPS)
                qn = q * inv * g
                x1, x2 = qn[MLA_NOPE:MLA_NOPE + half, :], qn[MLA_NOPE + half:, :]
                qt = jnp.concatenate([qn[:MLA_NOPE, :], x1 * cos - x2 * sin,
                                      x1 * sin + x2 * cos, pad], axis=0)
                qt_ref[hh] = (qt * scale).astype(BF16)
        return run

    q_steps = [q_latent] + [q_heads(h0) for h0 in range(0, MLA_HEADS, group)]
    if not with_kv:
        _interleave(q_steps, [])
        return

    per_head = MLA_NOPE + MLA_V

    def kv_latent():
        h = (xn * kvn_ref[...]).astype(BF16)
        cell["c_kv"] = _rms_rows(_dot(h, w_dc_ref[...]), lnorm_ref[...]).astype(BF16)
        kr = _dot_nt(w_drt_ref[...], h)
        cell["ss_rope"] = jnp.sum(kr * kr, axis=0, keepdims=True)
        krg = kr * kg_ref[MLA_NOPE:, :]
        x1, x2 = krg[:half, :], krg[half:, :]
        cell["r1"] = x1 * cos - x2 * sin
        cell["r2"] = x1 * sin + x2 * cos

    def kv_heads(h0):
        def run():
            r0, r1 = h0 * per_head, (h0 + group) * per_head
            kvt_ref[r0:r1, :] = _dot_nt(w_upt_ref[r0:r1, :], cell["c_kv"])
            g_nope = kg_ref[:MLA_NOPE, :]
            for hh in range(h0, h0 + group):
                kn = kvt_ref[hh * per_head:hh * per_head + MLA_NOPE, :]
                ss = jnp.sum(kn * kn, axis=0, keepdims=True) + cell["ss_rope"]
                inv = lax.rsqrt(ss * (1.0 / MLA_QK) + EPS)
                kt = jnp.concatenate([kn * inv * g_nope, cell["r1"] * inv, cell["r2"] * inv, pad],
                                     axis=0)
                k_ref[hh] = kt.T.astype(BF16)
                vt_ref[hh, 0] = kvt_ref[hh * per_head + MLA_NOPE:(hh + 1) * per_head, :].astype(BF16)
        return run

    kv_steps = [kv_latent] + [kv_heads(h0) for h0 in range(0, MLA_HEADS, group)]
    _interleave(kv_steps, q_steps)


def _mla_qkv(x, pos, inv_freq, q_params, kv_params=None):
    s, d = x.shape
    rows = PROJ_ROWS
    q_pre, w_dq, q_lat, w_uq, q_gain = q_params
    w_uqt = w_uq.T.astype(BF16)
    row_spec = pl.BlockSpec((rows, d), lambda i: (i, 0))
    in_specs = [row_spec, pl.BlockSpec((1, rows), lambda i: (0, i)), _full((MLA_ROPE // 2, 1)),
                _full((1, d)), _full(w_dq.shape), _full((1, w_dq.shape[1])), _full(w_uqt.shape),
                _full((MLA_QK, 1))]
    args = [x, pos, inv_freq, q_pre.reshape(1, d), w_dq.astype(BF16), q_lat.reshape(1, -1), w_uqt,
            q_gain.reshape(-1, 1)]
    out_shape = [jax.ShapeDtypeStruct((MLA_HEADS, QK_PAD, s), BF16)]
    out_specs = [pl.BlockSpec((MLA_HEADS, QK_PAD, rows), lambda i: (0, 0, i))]
    scratch = [pltpu.VMEM((w_uqt.shape[0], rows), F32)]
    if kv_params is not None:
        kv_pre, w_down, kv_lat, w_up, k_gain = kv_params
        w_dc = w_down[:, :MLA_KV_RANK].astype(BF16)
        w_drt = w_down[:, MLA_KV_RANK:].T.astype(BF16)
        w_upt = w_up.T.astype(BF16)
        in_specs += [_full((1, d)), _full(w_dc.shape), _full(w_drt.shape), _full((1, MLA_KV_RANK)),
                     _full(w_upt.shape), _full((MLA_QK, 1))]
        args += [kv_pre.reshape(1, d), w_dc, w_drt, kv_lat.reshape(1, -1), w_upt,
                 k_gain.reshape(-1, 1)]
        out_shape += [jax.ShapeDtypeStruct((MLA_HEADS, s, QK_PAD), BF16),
                      jax.ShapeDtypeStruct((MLA_HEADS, s // rows, MLA_V, rows), BF16)]
        out_specs += [pl.BlockSpec((MLA_HEADS, rows, QK_PAD), lambda i: (0, i, 0)),
                      pl.BlockSpec((MLA_HEADS, 1, MLA_V, rows), lambda i: (0, i, 0, 0))]
        scratch += [pltpu.VMEM((w_upt.shape[0], rows), F32)]
    kern = functools.partial(_qkv_kernel, rows=rows, with_kv=kv_params is not None)
    return pl.pallas_call(
        kern,
        out_shape=tuple(out_shape),
        grid=(s // rows,),
        in_specs=in_specs,
        out_specs=tuple(out_specs),
        scratch_shapes=scratch,
        compiler_params=_params(("parallel",)),
        name="mla_qkv" if kv_params is not None else "mla_q",
    )(*args)


def _attn_kernel(bounded_ref, qt_ref, k_ref, vt_ref, o_ref, acc_ref, l_ref, p0_ref, p1_ref, *,
                 tq, tk):
    i = pl.program_id(1)
    heads = qt_ref.shape[0]
    per_q = tq // tk
    assert per_q == 2
    key_c = lax.broadcasted_iota(jnp.int32, (tk, tk), 0) >> 6
    qry_c = lax.broadcasted_iota(jnp.int32, (tk, tk), 1) >> 6
    diag_mask = key_c <= qry_c

    def k_rows(hd, start, size):
        return k_ref[hd, pl.ds(pl.multiple_of(start, size), size), :]

    @pl.when(bounded_ref[0] != 0)
    def _():
        qts = [qt_ref[hd] for hd in range(heads)]

        def sublane_sums(p):
            return p.reshape(p.shape[0] // SUBLANES, SUBLANES, p.shape[1]).sum(axis=0)

        def weighted_values(hd, p_ref, j):
            return _dot(vt_ref[hd, j], p_ref[hd])

        j0, j1 = 2 * i, 2 * i + 1
        for hd in range(heads):
            p = jnp.where(diag_mask, jnp.exp2(_dot(k_rows(hd, j1 * tk, tk), qts[hd][:, tk:])), 0.0)
            p1_ref[hd] = jnp.concatenate([jnp.zeros((tk, tk), F32), p], axis=1).astype(BF16)
            l_ref[hd] = jnp.concatenate([jnp.zeros((SUBLANES, tk), F32), sublane_sums(p)], axis=1)
            acc_ref[hd] = jnp.zeros((MLA_V, tq), F32)

        def tile_pairs(t, n_pairs):
            half = tk // 2
            tot = [None] * heads
            l_tot = [None] * heads
            for u in range(n_pairs):
                prev = 2 * (t + u) - 1
                if u == 0:
                    prev = jnp.where(t == 0, j1, prev)
                stages = ((p0_ref, 2 * (t + u), p1_ref, prev),
                          (p1_ref, 2 * (t + u) + 1, p0_ref, 2 * (t + u)))
                for new_ref, j_new, old_ref, j_old in stages:
                    for hd in range(heads):
                        p_new = jnp.exp2(_dot(k_rows(hd, j_new * tk, tk), qts[hd]))
                        l_part = sublane_sums(p_new)
                        l_tot[hd] = l_part if l_tot[hd] is None else l_tot[hd] + l_part
                        for r0 in (0, half):
                            part = _dot(vt_ref[hd, j_old][:, r0:r0 + half],
                                        old_ref[hd, r0:r0 + half, :])
                            tot[hd] = part if tot[hd] is None else tot[hd] + part
                        new_ref[hd] = p_new.astype(BF16)
            for hd in range(heads):
                acc_ref[hd] += tot[hd]
                l_ref[hd] += l_tot[hd]

        done = 0
        for n_pairs, trips in ((4, i >> 2), (2, (i >> 1) & 1), (1, i & 1)):
            def body(u, carry, n_pairs=n_pairs, done=done):
                tile_pairs(done + n_pairs * u, n_pairs)
                return carry

            lax.fori_loop(0, trips, body, 0)
            done = done + n_pairs * trips

        j_last = jnp.where(i == 0, j1, 2 * i - 1)
        for hd in range(heads):
            p = jnp.exp2(_dot(k_rows(hd, j0 * tk, tk), qts[hd]))
            p = jnp.concatenate([jnp.where(diag_mask, p[:, :tk], 0.0), p[:, tk:]], axis=1)
            acc = (acc_ref[hd] + weighted_values(hd, p1_ref, j_last)
                   + _dot(vt_ref[hd, j0], p.astype(BF16)))
            l = jnp.sum(l_ref[hd] + sublane_sums(p), axis=0, keepdims=True)
            o_ref[hd * MLA_V:(hd + 1) * MLA_V, :] = (acc / l).astype(BF16)

    @pl.when(bounded_ref[0] == 0)
    def _():
        key_w = lax.broadcasted_iota(jnp.int32, (tk, tq), 0) >> 6
        qry_w = lax.broadcasted_iota(jnp.int32, (tk, tq), 1) >> 6
        for hd in range(heads):
            qt = qt_ref[hd]

            def step(j, carry, mask, hd=hd, qt=qt):
                m, l, acc = carry
                s = _dot(k_rows(hd, j * tk, tk), qt)
                if mask is not None:
                    s = jnp.where(mask, s, -jnp.inf)
                m_new = jnp.maximum(m, jnp.max(s, axis=0, keepdims=True))
                alpha = jnp.exp2(m - m_new)
                p = jnp.exp2(s - m_new)
                l = alpha * l + jnp.sum(p, axis=0, keepdims=True)
                acc = alpha * acc + _dot(vt_ref[hd, j], p.astype(BF16))
                return m_new, l, acc

            carry = (jnp.full((1, tq), -jnp.inf, F32), jnp.zeros((1, tq), F32),
                     jnp.zeros((MLA_V, tq), F32))
            carry = lax.fori_loop(0, i * per_q, lambda j, c, step=step: step(j, c, None), carry)
            for jj in range(per_q):
                carry = step(i * per_q + jj, carry, (key_w + jj * (tk // CHUNK)) <= qry_w)
            _, l, acc = carry
            o_ref[hd * MLA_V:(hd + 1) * MLA_V, :] = (acc / l).astype(BF16)


def _mla_attention(bounded, qt, k, vt):
    heads, _, s = qt.shape
    tq, tk, hb = ATTN_TQ, ATTN_TK, ATTN_HEADS_PER_STEP
    kern = functools.partial(_attn_kernel, tq=tq, tk=tk)
    return pl.pallas_call(
        kern,
        out_shape=jax.ShapeDtypeStruct((heads * MLA_V, s), BF16),
        grid_spec=pltpu.PrefetchScalarGridSpec(
            num_scalar_prefetch=1,
            grid=(heads // hb, s // tq),
            in_specs=[
                pl.BlockSpec((hb, QK_PAD, tq), lambda h, i, b: (h, 0, i)),
                pl.BlockSpec((hb, s, QK_PAD), lambda h, i, b: (h, 0, 0)),
                pl.BlockSpec((hb, s // tk, MLA_V, tk), lambda h, i, b: (h, 0, 0, 0)),
            ],
            out_specs=pl.BlockSpec((hb * MLA_V, tq), lambda h, i, b: (h, i)),
            scratch_shapes=[pltpu.VMEM((hb, MLA_V, tq), F32), pltpu.VMEM((hb, SUBLANES, tq), F32),
                            pltpu.VMEM((hb, tk, tq), BF16), pltpu.VMEM((hb, tk, tq), BF16)],
        ),
        compiler_params=_params(("parallel", "arbitrary")),
        name="mla_attn",
    )(bounded, qt, k, vt)


def kernel(x, positions, a_norm, a_w_in, a_w_gate_up, a_b_gate, a_out_norm, a_w_out, b_norm, b_w_dq,
           b_q_latent_norm, b_w_uq, b_q_norm, b_w_out, kv_norm, kv_w_down, kv_latent_norm, kv_w_up,
           k_norm, f_norm, f_w_in, f_w_out):
    batch, s, d = x.shape
    n_a = a_norm.shape[0]
    n_b = b_norm.shape[0]
    half = MLA_ROPE // 2
    inv_freq = (ROPE_THETA ** (-jnp.arange(half, dtype=F32) / half)).reshape(half, 1)
    kv_params = (kv_norm, kv_w_down, kv_latent_norm, kv_w_up, k_norm)

    def q_params(j):
        return b_norm[j], b_w_dq[j], b_q_latent_norm[j], b_w_uq[j], b_q_norm[j]

    outs = []
    for bi in range(batch):
        xb = x[bi]
        pos = positions[bi].reshape(1, s)
        qt_first = k_sh = vt_sh = None
        for layer in range(n_a + n_b):
            if layer < n_a:
                xb = _gla_layer(xb, a_norm[layer], a_w_in[layer], a_w_gate_up[layer],
                                a_b_gate[layer], a_out_norm[layer], a_w_out[layer])
                xb = _ffn(xb, f_norm[layer], f_w_in[layer], f_w_out[layer])
            else:
                j = layer - n_a
                qt = qt_first if j == 0 else _mla_qkv(xb, pos, inv_freq, q_params(j))[0]
                logit_bound = (LOGIT_BOUND_COEF * jnp.max(jnp.abs(b_q_norm[j]))
                               * jnp.max(jnp.abs(k_norm)))
                bounded = (logit_bound <= MAX_BOUNDED_LOGIT).astype(jnp.int32).reshape(1)
                attn_t = _mla_attention(bounded, qt, k_sh, vt_sh)
                xb = _ffn(xb, f_norm[layer], f_w_in[layer], f_w_out[layer],
                          attn_t=attn_t, w_o=b_w_out[j])
            if layer == n_a - 1 and n_b > 0:
                qt_first, k_sh, vt_sh = _mla_qkv(xb, pos, inv_freq, q_params(0), kv_params)
        outs.append(xb)
    return jnp.stack(outs, axis=0)
```

```python
import functools
import itertools

import jax
import jax.numpy as jnp
from jax import lax
from jax.experimental import pallas as pl
from jax.experimental.pallas import tpu as pltpu

F32 = jnp.float32
BF16 = jnp.bfloat16

EPS = 1e-6
CHUNK = 64
GLA_HEADS = 4
GLA_TAU = 16.0
GLA_GATE_RANK = 16
MLA_HEADS = 16
MLA_NOPE = 64
MLA_ROPE = 32
MLA_QK = MLA_NOPE + MLA_ROPE
MLA_V = 64
MLA_KV_RANK = 256
ROPE_THETA = 10000.0
LOG2_E = 1.4426950408889634

MXU_COLS = 256
LANES = 128
BF16_SUBLANES = 16
QK_PAD = LANES
V_AUG = MLA_V + BF16_SUBLANES
LOGIT_BOUND_COEF = 1.02 * LOG2_E * MLA_QK ** 0.5
MAX_BOUNDED_LOGIT = 100.0
V7X_VMEM_LIMIT_BYTES = 56 * 1024 * 1024

GLA_SUB_ROWS = 256
GLA_ROWS = 2 * GLA_SUB_ROWS
FFN_ROWS = 512
ATTN_TQ = 1024
ATTN_TK = ATTN_TQ // 2
ATTN_HEADS_PER_STEP = 2
PROJ_ROWS = ATTN_TK


def _dot(a, b):
    return jnp.dot(a, b, preferred_element_type=F32)


def _dot_nt(a, b):
    return lax.dot_general(a, b, (((1,), (1,)), ((), ())), preferred_element_type=F32)


def _dot_tn(a, b):
    return lax.dot_general(a, b, (((0,), (0,)), ((), ())), preferred_element_type=F32)


def _rms_rows(x, g):
    return x * lax.rsqrt(jnp.mean(x * x, axis=-1, keepdims=True) + EPS) * g


def _silu(x):
    return x * jax.nn.sigmoid(x)


def _params(semantics):
    return pltpu.CompilerParams(dimension_semantics=semantics,
                                vmem_limit_bytes=V7X_VMEM_LIMIT_BYTES)


def _interleave(first, second):
    for f, g in itertools.zip_longest(first, second):
        for step in (f, g):
            if step is not None:
                step()


def _full(shape):
    return pl.BlockSpec(shape, lambda *_: (0,) * len(shape), pipeline_mode=pl.Buffered(1))


def _gla_kernel(x_ref, xn_ref, norm_ref, w_main_ref, w_gl_ref, w_gate_ref, b_gate_ref, onorm_ref,
                w_out_ref, out_ref, state_ref, *scratch, sub, dk, dv):
    hk, hv = dk // GLA_HEADS, dv // GLA_HEADS
    subs = [scratch[0:3], scratch[3:6]]

    @pl.when(pl.program_id(0) == 0)
    def _():
        state_ref[...] = jnp.zeros_like(state_ref)

    r_i = lax.broadcasted_iota(jnp.int32, (sub, sub), 0)
    c_i = lax.broadcasted_iota(jnp.int32, (sub, sub), 1)
    tri = ((c_i <= r_i) & ((r_i >> 6) == (c_i >> 6))).astype(BF16)
    qi = lax.broadcasted_iota(jnp.int32, (CHUNK, CHUNK), 0)
    kj = lax.broadcasted_iota(jnp.int32, (CHUNK, CHUNK), 1)
    causal = kj <= qi
    q_scale = hk ** -0.5

    def project_pieces(x_rows, a):
        proj_ref, b_ref, _ = subs[a]
        cell = {}

        def norm():
            cell["h"] = _rms_rows(x_rows[...], norm_ref[...]).astype(BF16)

        def main_block(c0):
            def run():
                proj_ref[:, c0:c0 + MXU_COLS] = _dot(cell["h"], w_main_ref[:, c0:c0 + MXU_COLS])
            return run

        def gate_logits():
            gl = _dot(cell["h"], w_gl_ref[...]).astype(BF16)
            z = _dot(gl, w_gate_ref[...]) + b_gate_ref[...]
            cell["log_a"] = ((jnp.minimum(z, 0.0) - jnp.log1p(jnp.exp(-jnp.abs(z))))
                             * (1.0 / GLA_TAU))

        def cumulate():
            log_a = cell["log_a"]
            hi = log_a.astype(BF16)
            rest = log_a - hi.astype(F32)
            mid = rest.astype(BF16)
            lo = (rest - mid.astype(F32)).astype(BF16)
            b_ref[...] = _dot(tri, hi) + _dot(tri, mid) + _dot(tri, lo)

        blocks = [main_block(c0) for c0 in range(0, 2 * dk + 2 * dv, MXU_COLS)]
        return [norm, gate_logits] + blocks[:2] + [cumulate] + blocks[2:]

    def recur_pieces(a):
        proj_ref, b_ref, o_ref = subs[a]
        cell = {}

        def prepare(c):
            r0 = c * CHUNK
            b_c = b_ref[r0:r0 + CHUNK, :]
            b_last = b_c[CHUNK - 1:CHUNK, :]
            q_c = proj_ref[r0:r0 + CHUNK, 0:dk]
            k_c = proj_ref[r0:r0 + CHUNK, dk:2 * dk]
            cell[c] = dict(
                v=proj_ref[r0:r0 + CHUNK, 2 * dk:2 * dk + dv].astype(BF16),
                q_dec=(q_c * q_scale * jnp.exp(b_c)).astype(BF16),
                k_inv=(k_c * jnp.exp(-b_c)).astype(BF16),
                k_end=(k_c * jnp.exp(b_last - b_c)).astype(BF16),
                decay=jnp.exp(b_last))

        def unit(c, hh):
            def run():
                if hh == 0:
                    prepare(c)
                t = cell[c]
                r0 = c * CHUNK
                ks = slice(hh * hk, (hh + 1) * hk)
                vs = slice(hh * hv, (hh + 1) * hv)
                attn = jnp.where(causal, _dot_nt(t["q_dec"][:, ks], t["k_inv"][:, ks]),
                                 0.0).astype(BF16)
                state_t = state_ref[hh]
                o_h = (_dot(attn, t["v"][:, vs])
                       + _dot_nt(t["q_dec"][:, ks], state_t.astype(BF16)))
                state_ref[hh] = (t["decay"][:, ks] * state_t
                                 + _dot_tn(t["v"][:, vs], t["k_end"][:, ks]))
                o_ref[r0:r0 + CHUNK, vs] = _rms_rows(o_h, onorm_ref[...])
            return run

        return [unit(c, hh) for c in range(sub // CHUNK) for hh in range(GLA_HEADS)]

    def output_pieces(a):
        proj_ref, _, o_ref = subs[a]
        rows = slice(a * sub, (a + 1) * sub)
        cell = {}

        def gate():
            g = _silu(proj_ref[:, 2 * dk + dv:2 * dk + 2 * dv])
            cell["g"] = (o_ref[...] * g).astype(BF16)

        def out_block(c0):
            def run():
                cols = slice(c0, c0 + MXU_COLS)
                out_ref[rows, cols] = x_ref[rows, cols] + _dot(cell["g"], w_out_ref[:, cols])
            return run

        return [gate] + [out_block(c0) for c0 in range(0, out_ref.shape[1], MXU_COLS)]

    @pl.when(pl.program_id(0) == 0)
    def _():
        _interleave(project_pieces(x_ref.at[0:sub, :], 0), [])

    next_proj = project_pieces(xn_ref, 0)
    out0, out1 = output_pieces(0), output_pieces(1)
    split = len(next_proj) - len(out1) - 1
    _interleave(recur_pieces(0), project_pieces(x_ref.at[sub:2 * sub, :], 1))
    _interleave(recur_pieces(1), out0 + next_proj[:split])
    _interleave(out1, next_proj[split:])


def _gla_layer(x, norm, w_in, w_gate_up, b_gate, out_norm, w_out):
    s, d = x.shape
    dk = w_gate_up.shape[1]
    dv = (w_in.shape[1] - 2 * dk - GLA_GATE_RANK) // 2
    n_main = 2 * dk + 2 * dv
    rows = GLA_ROWS
    w_main = w_in[:, :n_main].astype(BF16)
    w_gl = jnp.pad(w_in[:, n_main:], ((0, 0), (0, LANES - GLA_GATE_RANK))).astype(BF16)
    w_gate = jnp.pad(w_gate_up, ((0, LANES - GLA_GATE_RANK), (0, 0))).astype(BF16)
    sub = GLA_SUB_ROWS
    last_sub = s // sub - 1
    sub_scratch = [pltpu.VMEM((sub, n_main), F32), pltpu.VMEM((sub, dk), F32),
                   pltpu.VMEM((sub, dv), F32)]
    kern = functools.partial(_gla_kernel, sub=sub, dk=dk, dv=dv)
    return pl.pallas_call(
        kern,
        out_shape=jax.ShapeDtypeStruct((s, d), F32),
        grid=(s // rows,),
        in_specs=[
            pl.BlockSpec((rows, d), lambda i: (i, 0)),
            pl.BlockSpec((sub, d), lambda i: (jnp.minimum(2 * i + 2, last_sub), 0)),
            _full((1, d)),
            _full((d, n_main)),
            _full((d, LANES)),
            _full((LANES, dk)),
            _full((1, dk)),
            _full((1, dv // GLA_HEADS)),
            _full((dv, d)),
        ],
        out_specs=pl.BlockSpec((rows, d), lambda i: (i, 0)),
        scratch_shapes=[pltpu.VMEM((GLA_HEADS, dv // GLA_HEADS, dk // GLA_HEADS), F32)]
        + sub_scratch * 2,
        compiler_params=_params(("arbitrary",)),
        name="gla_layer",
    )(x, x, norm.reshape(1, d), w_main, w_gl, w_gate, b_gate.reshape(1, dk),
      out_norm.reshape(1, -1), w_out.astype(BF16))


def _ffn_body(x, norm_ref, w_in_ref, w_out_ref, out_ref, hidden):
    h = _rms_rows(x, norm_ref[...]).astype(BF16)
    gate = _dot(h, w_in_ref[:, 0:hidden])
    up = _dot(h, w_in_ref[:, hidden:2 * hidden])
    act = (_silu(gate) * up).astype(BF16)
    out_ref[...] = x + _dot(act, w_out_ref[...])


def _ffn_kernel(x_ref, norm_ref, w_in_ref, w_out_ref, out_ref, *, hidden):
    _ffn_body(x_ref[...], norm_ref, w_in_ref, w_out_ref, out_ref, hidden)


def _proj_ffn_kernel(x_ref, ot_ref, w_o_ref, norm_ref, w_in_ref, w_out_ref, out_ref, *, hidden):
    x = x_ref[...] + _dot_tn(ot_ref[...], w_o_ref[...])
    _ffn_body(x, norm_ref, w_in_ref, w_out_ref, out_ref, hidden)


def _ffn(x, norm, w_in, w_out, attn_t=None, w_o=None):
    s, d = x.shape
    hidden = w_out.shape[0]
    rows = FFN_ROWS
    row_spec = pl.BlockSpec((rows, d), lambda i: (i, 0))
    ffn_specs = [_full((1, d)), _full((d, 2 * hidden)), _full((hidden, d))]
    ffn_args = (norm.reshape(1, d), w_in.astype(BF16), w_out.astype(BF16))
    if attn_t is None:
        kern = functools.partial(_ffn_kernel, hidden=hidden)
        in_specs = [row_spec] + ffn_specs
        args = (x,) + ffn_args
        name = "ffn"
    else:
        kern = functools.partial(_proj_ffn_kernel, hidden=hidden)
        in_specs = [row_spec, pl.BlockSpec((attn_t.shape[0], rows), lambda i: (0, i)),
                    _full(w_o.shape)] + ffn_specs
        args = (x, attn_t, w_o.astype(BF16)) + ffn_args
        name = "attn_proj_ffn"
    return pl.pallas_call(
        kern,
        out_shape=jax.ShapeDtypeStruct((s, d), F32),
        grid=(s // rows,),
        in_specs=in_specs,
        out_specs=row_spec,
        compiler_params=_params(("parallel",)),
        name=name,
    )(*args)


def _qkv_kernel(*refs, rows, with_kv):
    if with_kv:
        (x_ref, pos_ref, inv_freq_ref, qn_ref, w_dq_ref, qln_ref, w_uqt_ref, qg_ref,
         kvn_ref, w_dc_ref, w_drt_ref, lnorm_ref, w_upt_ref, kg_ref,
         qt_ref, k_ref, vt_ref, qall_ref, kvt_ref) = refs
    else:
        (x_ref, pos_ref, inv_freq_ref, qn_ref, w_dq_ref, qln_ref, w_uqt_ref, qg_ref,
         qt_ref, qall_ref) = refs
    half = MLA_ROPE // 2
    group = 4
    x = x_ref[...]
    xn = x * lax.rsqrt(jnp.mean(x * x, axis=-1, keepdims=True) + EPS)
    ang = pos_ref[...].astype(F32) * inv_freq_ref[...]
    cos, sin = jnp.cos(ang), jnp.sin(ang)
    pad = jnp.zeros((QK_PAD - MLA_QK, rows), F32)
    cell = {}

    def q_latent():
        h = (xn * qn_ref[...]).astype(BF16)
        cell["c_q"] = _rms_rows(_dot(h, w_dq_ref[...]), qln_ref[...]).astype(BF16)

    def q_heads(h0):
        def run():
            r0, r1 = h0 * MLA_QK, (h0 + group) * MLA_QK
            qall_ref[r0:r1, :] = _dot_nt(w_uqt_ref[r0:r1, :], cell["c_q"])
            g = qg_ref[...]
            scale = MLA_QK ** -0.5 * LOG2_E
            for hh in range(h0, h0 + group):
                q = qall_ref[hh * MLA_QK:(hh + 1) * MLA_QK, :]
                inv = lax.rsqrt(jnp.mean(q * q, axis=0, keepdims=True) + EPS)
                qn = q * inv * g
                x1, x2 = qn[MLA_NOPE:MLA_NOPE + half, :], qn[MLA_NOPE + half:, :]
                qt = jnp.concatenate([qn[:MLA_NOPE, :], x1 * cos - x2 * sin,
                                      x1 * sin + x2 * cos, pad], axis=0)
                qt_ref[hh] = (qt * scale).astype(BF16)
        return run

    q_steps = [q_latent] + [q_heads(h0) for h0 in range(0, MLA_HEADS, group)]
    if not with_kv:
        _interleave(q_steps, [])
        return

    per_head = MLA_NOPE + MLA_V
    ones_row = (lax.broadcasted_iota(jnp.int32, (V_AUG - MLA_V, rows), 0) == 0).astype(F32)

    def kv_latent():
        h = (xn * kvn_ref[...]).astype(BF16)
        cell["c_kv"] = _rms_rows(_dot(h, w_dc_ref[...]), lnorm_ref[...]).astype(BF16)
        kr = _dot_nt(w_drt_ref[...], h)
        cell["ss_rope"] = jnp.sum(kr * kr, axis=0, keepdims=True)
        krg = kr * kg_ref[MLA_NOPE:, :]
        x1, x2 = krg[:half, :], krg[half:, :]
        cell["r1"] = x1 * cos - x2 * sin
        cell["r2"] = x1 * sin + x2 * cos

    def kv_heads(h0):
        def run():
            r0, r1 = h0 * per_head, (h0 + group) * per_head
            kvt_ref[r0:r1, :] = _dot_nt(w_upt_ref[r0:r1, :], cell["c_kv"])
            g_nope = kg_ref[:MLA_NOPE, :]
            for hh in range(h0, h0 + group):
                kn = kvt_ref[hh * per_head:hh * per_head + MLA_NOPE, :]
                ss = jnp.sum(kn * kn, axis=0, keepdims=True) + cell["ss_rope"]
                inv = lax.rsqrt(ss * (1.0 / MLA_QK) + EPS)
                kt = jnp.concatenate([kn * inv * g_nope, cell["r1"] * inv, cell["r2"] * inv, pad],
                                     axis=0)
                k_ref[hh] = kt.T.astype(BF16)
                vt = kvt_ref[hh * per_head + MLA_NOPE:(hh + 1) * per_head, :]
                vt_ref[hh, 0] = jnp.concatenate([vt, ones_row], axis=0).astype(BF16)
        return run

    kv_steps = [kv_latent] + [kv_heads(h0) for h0 in range(0, MLA_HEADS, group)]
    _interleave(kv_steps, q_steps)


def _mla_qkv(x, pos, inv_freq, q_params, kv_params=None):
    s, d = x.shape
    rows = PROJ_ROWS
    q_pre, w_dq, q_lat, w_uq, q_gain = q_params
    w_uqt = w_uq.T.astype(BF16)
    row_spec = pl.BlockSpec((rows, d), lambda i: (i, 0))
    in_specs = [row_spec, pl.BlockSpec((1, rows), lambda i: (0, i)), _full((MLA_ROPE // 2, 1)),
                _full((1, d)), _full(w_dq.shape), _full((1, w_dq.shape[1])), _full(w_uqt.shape),
                _full((MLA_QK, 1))]
    args = [x, pos, inv_freq, q_pre.reshape(1, d), w_dq.astype(BF16), q_lat.reshape(1, -1), w_uqt,
            q_gain.reshape(-1, 1)]
    out_shape = [jax.ShapeDtypeStruct((MLA_HEADS, QK_PAD, s), BF16)]
    out_specs = [pl.BlockSpec((MLA_HEADS, QK_PAD, rows), lambda i: (0, 0, i))]
    scratch = [pltpu.VMEM((w_uqt.shape[0], rows), F32)]
    if kv_params is not None:
        kv_pre, w_down, kv_lat, w_up, k_gain = kv_params
        w_dc = w_down[:, :MLA_KV_RANK].astype(BF16)
        w_drt = w_down[:, MLA_KV_RANK:].T.astype(BF16)
        w_upt = w_up.T.astype(BF16)
        in_specs += [_full((1, d)), _full(w_dc.shape), _full(w_drt.shape), _full((1, MLA_KV_RANK)),
                     _full(w_upt.shape), _full((MLA_QK, 1))]
        args += [kv_pre.reshape(1, d), w_dc, w_drt, kv_lat.reshape(1, -1), w_upt,
                 k_gain.reshape(-1, 1)]
        out_shape += [jax.ShapeDtypeStruct((MLA_HEADS, s, QK_PAD), BF16),
                      jax.ShapeDtypeStruct((MLA_HEADS, s // rows, V_AUG, rows), BF16)]
        out_specs += [pl.BlockSpec((MLA_HEADS, rows, QK_PAD), lambda i: (0, i, 0)),
                      pl.BlockSpec((MLA_HEADS, 1, V_AUG, rows), lambda i: (0, i, 0, 0))]
        scratch += [pltpu.VMEM((w_upt.shape[0], rows), F32)]
    kern = functools.partial(_qkv_kernel, rows=rows, with_kv=kv_params is not None)
    return pl.pallas_call(
        kern,
        out_shape=tuple(out_shape),
        grid=(s // rows,),
        in_specs=in_specs,
        out_specs=tuple(out_specs),
        scratch_shapes=scratch,
        compiler_params=_params(("parallel",)),
        name="mla_qkv" if kv_params is not None else "mla_q",
    )(*args)


def _attn_kernel(bounded_ref, qt_ref, k_ref, vt_ref, o_ref, acc_ref, p0_ref, p1_ref, *, tq, tk):
    i = pl.program_id(1)
    heads = qt_ref.shape[0]
    per_q = tq // tk
    assert per_q == 2
    key_c = lax.broadcasted_iota(jnp.int32, (tk, tk), 0) >> 6
    qry_c = lax.broadcasted_iota(jnp.int32, (tk, tk), 1) >> 6
    diag_mask = key_c <= qry_c

    def k_rows(hd, start, size):
        return k_ref[hd, pl.ds(pl.multiple_of(start, size), size), :]

    @pl.when(bounded_ref[0] != 0)
    def _():
        qts = [qt_ref[hd] for hd in range(heads)]

        def probs(k, qt):
            return jnp.exp2(_dot(k, qt).astype(BF16))

        def weighted_values(hd, p_ref, j):
            return _dot(vt_ref[hd, j], p_ref[hd])

        j0, j1 = 2 * i, 2 * i + 1
        for hd in range(heads):
            p = jnp.where(diag_mask, probs(k_rows(hd, j1 * tk, tk), qts[hd][:, tk:]), 0.0)
            p1_ref[hd] = jnp.concatenate([jnp.zeros((tk, tk), BF16), p.astype(BF16)], axis=1)
            acc_ref[hd] = jnp.zeros((V_AUG, tq), F32)

        def tile_pairs(t, n_pairs):
            half = tk // 2
            tot = [None] * heads
            for u in range(n_pairs):
                prev = 2 * (t + u) - 1
                if u == 0:
                    prev = jnp.where(t == 0, j1, prev)
                stages = ((p0_ref, 2 * (t + u), p1_ref, prev),
                          (p1_ref, 2 * (t + u) + 1, p0_ref, 2 * (t + u)))
                for new_ref, j_new, old_ref, j_old in stages:
                    for hd in range(heads):
                        p_new = probs(k_rows(hd, j_new * tk, tk), qts[hd])
                        for r0 in (0, half):
                            part = _dot(vt_ref[hd, j_old][:, r0:r0 + half],
                                        old_ref[hd, r0:r0 + half, :])
                            tot[hd] = part if tot[hd] is None else tot[hd] + part
                        new_ref[hd] = p_new
            for hd in range(heads):
                acc_ref[hd] += tot[hd]

        done = 0
        for n_pairs, trips in ((4, i >> 2), (2, (i >> 1) & 1), (1, i & 1)):
            def body(u, carry, n_pairs=n_pairs, done=done):
                tile_pairs(done + n_pairs * u, n_pairs)
                return carry

            lax.fori_loop(0, trips, body, 0)
            done = done + n_pairs * trips

        j_last = jnp.where(i == 0, j1, 2 * i - 1)
        for hd in range(heads):
            p = probs(k_rows(hd, j0 * tk, tk), qts[hd])
            p = jnp.concatenate([jnp.where(diag_mask, p[:, :tk], 0.0).astype(BF16), p[:, tk:]],
                                axis=1)
            acc = (acc_ref[hd] + weighted_values(hd, p1_ref, j_last) + _dot(vt_ref[hd, j0], p))
            o_ref[hd * MLA_V:(hd + 1) * MLA_V, :] = (acc[:MLA_V, :]
                                                     / acc[MLA_V:MLA_V + 1, :]).astype(BF16)

    @pl.when(bounded_ref[0] == 0)
    def _():
        key_w = lax.broadcasted_iota(jnp.int32, (tk, tq), 0) >> 6
        qry_w = lax.broadcasted_iota(jnp.int32, (tk, tq), 1) >> 6
        for hd in range(heads):
            qt = qt_ref[hd]

            def step(j, carry, mask, hd=hd, qt=qt):
                m, l, acc = carry
                s = _dot(k_rows(hd, j * tk, tk), qt)
                if mask is not None:
                    s = jnp.where(mask, s, -jnp.inf)
                m_new = jnp.maximum(m, jnp.max(s, axis=0, keepdims=True))
                alpha = jnp.exp2(m - m_new)
                p = jnp.exp2(s - m_new)
                l = alpha * l + jnp.sum(p, axis=0, keepdims=True)
                acc = alpha * acc + _dot(vt_ref[hd, j][:MLA_V, :], p.astype(BF16))
                return m_new, l, acc

            carry = (jnp.full((1, tq), -jnp.inf, F32), jnp.zeros((1, tq), F32),
                     jnp.zeros((MLA_V, tq), F32))
            carry = lax.fori_loop(0, i * per_q, lambda j, c, step=step: step(j, c, None), carry)
            for jj in range(per_q):
                carry = step(i * per_q + jj, carry, (key_w + jj * (tk // CHUNK)) <= qry_w)
            _, l, acc = carry
            o_ref[hd * MLA_V:(hd + 1) * MLA_V, :] = (acc / l).astype(BF16)


def _mla_attention(bounded, qt, k, vt):
    heads, _, s = qt.shape
    tq, tk, hb = ATTN_TQ, ATTN_TK, ATTN_HEADS_PER_STEP
    kern = functools.partial(_attn_kernel, tq=tq, tk=tk)
    return pl.pallas_call(
        kern,
        out_shape=jax.ShapeDtypeStruct((heads * MLA_V, s), BF16),
        grid_spec=pltpu.PrefetchScalarGridSpec(
            num_scalar_prefetch=1,
            grid=(heads // hb, s // tq),
            in_specs=[
                pl.BlockSpec((hb, QK_PAD, tq), lambda h, i, b: (h, 0, i)),
                pl.BlockSpec((hb, s, QK_PAD), lambda h, i, b: (h, 0, 0)),
                pl.BlockSpec((hb, s // tk, V_AUG, tk), lambda h, i, b: (h, 0, 0, 0)),
            ],
            out_specs=pl.BlockSpec((hb * MLA_V, tq), lambda h, i, b: (h, i)),
            scratch_shapes=[pltpu.VMEM((hb, V_AUG, tq), F32), pltpu.VMEM((hb, tk, tq), BF16),
                            pltpu.VMEM((hb, tk, tq), BF16)],
        ),
        compiler_params=_params(("parallel", "arbitrary")),
        name="mla_attn",
    )(bounded, qt, k, vt)


def kernel(x, positions, a_norm, a_w_in, a_w_gate_up, a_b_gate, a_out_norm, a_w_out, b_norm, b_w_dq,
           b_q_latent_norm, b_w_uq, b_q_norm, b_w_out, kv_norm, kv_w_down, kv_latent_norm, kv_w_up,
           k_norm, f_norm, f_w_in, f_w_out):
    batch, s, d = x.shape
    n_a = a_norm.shape[0]
    n_b = b_norm.shape[0]
    half = MLA_ROPE // 2
    inv_freq = (ROPE_THETA ** (-jnp.arange(half, dtype=F32) / half)).reshape(half, 1)
    kv_params = (kv_norm, kv_w_down, kv_latent_norm, kv_w_up, k_norm)

    def q_params(j):
        return b_norm[j], b_w_dq[j], b_q_latent_norm[j], b_w_uq[j], b_q_norm[j]

    outs = []
    for bi in range(batch):
        xb = x[bi]
        pos = positions[bi].reshape(1, s)
        qt_first = k_sh = vt_sh = None
        for layer in range(n_a + n_b):
            if layer < n_a:
                xb = _gla_layer(xb, a_norm[layer], a_w_in[layer], a_w_gate_up[layer],
                                a_b_gate[layer], a_out_norm[layer], a_w_out[layer])
                xb = _ffn(xb, f_norm[layer], f_w_in[layer], f_w_out[layer])
            else:
                j = layer - n_a
                qt = qt_first if j == 0 else _mla_qkv(xb, pos, inv_freq, q_params(j))[0]
                logit_bound = (LOGIT_BOUND_COEF * jnp.max(jnp.abs(b_q_norm[j]))
                               * jnp.max(jnp.abs(k_norm)))
                bounded = (logit_bound <= MAX_BOUNDED_LOGIT).astype(jnp.int32).reshape(1)
                attn_t = _mla_attention(bounded, qt, k_sh, vt_sh)
                xb = _ffn(xb, f_norm[layer], f_w_in[layer], f_w_out[layer],
                          attn_t=attn_t, w_o=b_w_out[j])
            if layer == n_a - 1 and n_b > 0:
                qt_first, k_sh, vt_sh = _mla_qkv(xb, pos, inv_freq, q_params(0), kv_params)
        outs.append(xb)
    return jnp.stack(outs, axis=0)
```

```python
import functools
import itertools

import jax
import jax.numpy as jnp
from jax import lax
from jax.experimental import pallas as pl
from jax.experimental.pallas import tpu as pltpu

F32 = jnp.float32
BF16 = jnp.bfloat16

EPS = 1e-6
CHUNK = 64
GLA_HEADS = 4
GLA_TAU = 16.0
GLA_GATE_RANK = 16
MLA_HEADS = 16
MLA_NOPE = 64
MLA_ROPE = 32
MLA_QK = MLA_NOPE + MLA_ROPE
MLA_V = 64
MLA_KV_RANK = 256
ROPE_THETA = 10000.0
LOG2_E = 1.4426950408889634

MXU_COLS = 256
LANES = 128
SUBLANES = 8
QK_PAD = LANES
LOGIT_BOUND_COEF = 1.02 * LOG2_E * MLA_QK ** 0.5
MAX_BOUNDED_LOGIT = 100.0
V7X_VMEM_LIMIT_BYTES = 56 * 1024 * 1024

GLA_SUB_ROWS = 256
GLA_ROWS = 2 * GLA_SUB_ROWS
FFN_ROWS = 512
ATTN_TQ = 1024
ATTN_TK = ATTN_TQ // 2
ATTN_HEADS_PER_STEP = 2
PROJ_ROWS = ATTN_TK


def _dot(a, b):
    return jnp.dot(a, b, preferred_element_type=F32)


def _dot_nt(a, b):
    return lax.dot_general(a, b, (((1,), (1,)), ((), ())), preferred_element_type=F32)


def _dot_tn(a, b):
    return lax.dot_general(a, b, (((0,), (0,)), ((), ())), preferred_element_type=F32)


def _rms_rows(x, g):
    return x * lax.rsqrt(jnp.mean(x * x, axis=-1, keepdims=True) + EPS) * g


def _silu(x):
    return x * jax.nn.sigmoid(x)


def _params(semantics):
    return pltpu.CompilerParams(dimension_semantics=semantics,
                                vmem_limit_bytes=V7X_VMEM_LIMIT_BYTES)


def _alternate(first, second):
    pairs = itertools.zip_longest(first, second)
    return [step for pair in pairs for step in pair if step is not None]


def _interleave(first, second):
    for step in _alternate(first, second):
        step()


def _full(shape):
    return pl.BlockSpec(shape, lambda *_: (0,) * len(shape), pipeline_mode=pl.Buffered(1))


def _gla_kernel(x_ref, xn_ref, norm_ref, w_main_ref, w_gl_ref, w_gate_ref, b_gate_ref, onorm_ref,
                w_out_ref, out_ref, state_ref, *scratch, sub, dk, dv):
    hk, hv = dk // GLA_HEADS, dv // GLA_HEADS
    subs = [scratch[0:3], scratch[3:6]]

    @pl.when(pl.program_id(0) == 0)
    def _():
        state_ref[...] = jnp.zeros_like(state_ref)

    r_i = lax.broadcasted_iota(jnp.int32, (sub, sub), 0)
    c_i = lax.broadcasted_iota(jnp.int32, (sub, sub), 1)
    tri = ((c_i <= r_i) & ((r_i >> 6) == (c_i >> 6))).astype(BF16)
    qi = lax.broadcasted_iota(jnp.int32, (CHUNK, CHUNK), 0)
    kj = lax.broadcasted_iota(jnp.int32, (CHUNK, CHUNK), 1)
    causal = kj <= qi
    q_scale = hk ** -0.5

    def project_pieces(x_rows, a):
        proj_ref, b_ref, _ = subs[a]
        cell = {}

        def norm():
            cell["h"] = _rms_rows(x_rows[...], norm_ref[...]).astype(BF16)

        def main_block(c0):
            def run():
                proj_ref[:, c0:c0 + MXU_COLS] = _dot(cell["h"], w_main_ref[:, c0:c0 + MXU_COLS])
            return run

        def gate_logits():
            gl = _dot(cell["h"], w_gl_ref[...]).astype(BF16)
            z = _dot(gl, w_gate_ref[...]) + b_gate_ref[...]
            cell["log_a"] = ((jnp.minimum(z, 0.0) - jnp.log1p(jnp.exp(-jnp.abs(z))))
                             * (1.0 / GLA_TAU))

        def cumulate():
            log_a = cell["log_a"]
            hi = log_a.astype(BF16)
            rest = log_a - hi.astype(F32)
            mid = rest.astype(BF16)
            lo = (rest - mid.astype(F32)).astype(BF16)
            b_ref[...] = _dot(tri, hi) + _dot(tri, mid) + _dot(tri, lo)

        blocks = [main_block(c0) for c0 in range(0, 2 * dk + 2 * dv, MXU_COLS)]
        return [norm, gate_logits] + blocks[:2] + [cumulate] + blocks[2:]

    def recur_pieces(a):
        proj_ref, b_ref, o_ref = subs[a]
        cell = {}

        def prepare(c):
            r0 = c * CHUNK
            b_c = b_ref[r0:r0 + CHUNK, :]
            b_last = b_c[CHUNK - 1:CHUNK, :]
            q_c = proj_ref[r0:r0 + CHUNK, 0:dk]
            k_c = proj_ref[r0:r0 + CHUNK, dk:2 * dk]
            cell[c] = dict(
                v=proj_ref[r0:r0 + CHUNK, 2 * dk:2 * dk + dv].astype(BF16),
                q_dec=(q_c * q_scale * jnp.exp(b_c)).astype(BF16),
                k_inv=(k_c * jnp.exp(-b_c)).astype(BF16),
                k_end=(k_c * jnp.exp(b_last - b_c)).astype(BF16),
                decay=jnp.exp(b_last))

        def unit(c, hh):
            def run():
                if hh == 0:
                    prepare(c)
                t = cell[c]
                r0 = c * CHUNK
                ks = slice(hh * hk, (hh + 1) * hk)
                vs = slice(hh * hv, (hh + 1) * hv)
                attn = jnp.where(causal, _dot_nt(t["q_dec"][:, ks], t["k_inv"][:, ks]),
                                 0.0).astype(BF16)
                state_t = state_ref[hh]
                o_h = (_dot(attn, t["v"][:, vs])
                       + _dot_nt(t["q_dec"][:, ks], state_t.astype(BF16)))
                state_ref[hh] = (t["decay"][:, ks] * state_t
                                 + _dot_tn(t["v"][:, vs], t["k_end"][:, ks]))
                o_ref[r0:r0 + CHUNK, vs] = _rms_rows(o_h, onorm_ref[...])
            return run

        return [unit(c, hh) for c in range(sub // CHUNK) for hh in range(GLA_HEADS)]

    def output_pieces(a):
        proj_ref, _, o_ref = subs[a]
        rows = slice(a * sub, (a + 1) * sub)
        cell = {}

        def gate():
            g = _silu(proj_ref[:, 2 * dk + dv:2 * dk + 2 * dv])
            cell["g"] = (o_ref[...] * g).astype(BF16)

        def out_block(c0):
            def run():
                cols = slice(c0, c0 + MXU_COLS)
                out_ref[rows, cols] = x_ref[rows, cols] + _dot(cell["g"], w_out_ref[:, cols])
            return run

        return [gate] + [out_block(c0) for c0 in range(0, out_ref.shape[1], MXU_COLS)]

    @pl.when(pl.program_id(0) == 0)
    def _():
        _interleave(project_pieces(x_ref.at[0:sub, :], 0), [])

    next_proj = project_pieces(xn_ref, 0)
    out0, out1 = output_pieces(0), output_pieces(1)
    split = len(next_proj) - len(out1) - 1
    _interleave(recur_pieces(0), project_pieces(x_ref.at[sub:2 * sub, :], 1))
    _interleave(recur_pieces(1), out0 + next_proj[:split])
    _interleave(out1, next_proj[split:])


def _gla_layer(x, norm, w_in, w_gate_up, b_gate, out_norm, w_out):
    s, d = x.shape
    dk = w_gate_up.shape[1]
    dv = (w_in.shape[1] - 2 * dk - GLA_GATE_RANK) // 2
    n_main = 2 * dk + 2 * dv
    rows = GLA_ROWS
    w_main = w_in[:, :n_main].astype(BF16)
    w_gl = jnp.pad(w_in[:, n_main:], ((0, 0), (0, LANES - GLA_GATE_RANK))).astype(BF16)
    w_gate = jnp.pad(w_gate_up, ((0, LANES - GLA_GATE_RANK), (0, 0))).astype(BF16)
    sub = GLA_SUB_ROWS
    last_sub = s // sub - 1
    sub_scratch = [pltpu.VMEM((sub, n_main), F32), pltpu.VMEM((sub, dk), F32),
                   pltpu.VMEM((sub, dv), F32)]
    kern = functools.partial(_gla_kernel, sub=sub, dk=dk, dv=dv)
    return pl.pallas_call(
        kern,
        out_shape=jax.ShapeDtypeStruct((s, d), F32),
        grid=(s // rows,),
        in_specs=[
            pl.BlockSpec((rows, d), lambda i: (i, 0)),
            pl.BlockSpec((sub, d), lambda i: (jnp.minimum(2 * i + 2, last_sub), 0)),
            _full((1, d)),
            _full((d, n_main)),
            _full((d, LANES)),
            _full((LANES, dk)),
            _full((1, dk)),
            _full((1, dv // GLA_HEADS)),
            _full((dv, d)),
        ],
        out_specs=pl.BlockSpec((rows, d), lambda i: (i, 0)),
        scratch_shapes=[pltpu.VMEM((GLA_HEADS, dv // GLA_HEADS, dk // GLA_HEADS), F32)]
        + sub_scratch * 2,
        compiler_params=_params(("arbitrary",)),
        name="gla_layer",
    )(x, x, norm.reshape(1, d), w_main, w_gl, w_gate, b_gate.reshape(1, dk),
      out_norm.reshape(1, -1), w_out.astype(BF16))


def _ffn_body(x, norm_ref, w_in_ref, w_out_ref, out_ref, hidden):
    h = _rms_rows(x, norm_ref[...]).astype(BF16)
    gate = _dot(h, w_in_ref[:, 0:hidden])
    up = _dot(h, w_in_ref[:, hidden:2 * hidden])
    act = (_silu(gate) * up).astype(BF16)
    out_ref[...] = x + _dot(act, w_out_ref[...])


def _ffn_kernel(x_ref, norm_ref, w_in_ref, w_out_ref, out_ref, *, hidden):
    _ffn_body(x_ref[...], norm_ref, w_in_ref, w_out_ref, out_ref, hidden)


def _proj_ffn_kernel(x_ref, ot_ref, w_o_ref, norm_ref, w_in_ref, w_out_ref, out_ref, *, hidden):
    x = x_ref[...] + _dot_tn(ot_ref[...], w_o_ref[...])
    _ffn_body(x, norm_ref, w_in_ref, w_out_ref, out_ref, hidden)


def _ffn_steps(x_ref, norm_ref, w_in_ref, w_out_ref, out_refs, act_ref, hidden):
    cell = {}

    def norm():
        cell["h"] = _rms_rows(x_ref[...], norm_ref[...]).astype(BF16)

    def hidden_block(c0):
        def run():
            gate = _dot(cell["h"], w_in_ref[:, c0:c0 + MXU_COLS])
            up = _dot(cell["h"], w_in_ref[:, hidden + c0:hidden + c0 + MXU_COLS])
            act_ref[:, c0:c0 + MXU_COLS] = (_silu(gate) * up).astype(BF16)
        return run

    def out_block(c0):
        def run():
            cols = slice(c0, c0 + MXU_COLS)
            y = x_ref[:, cols] + _dot(act_ref[...], w_out_ref[:, cols])
            for ref in out_refs:
                ref[:, cols] = y
        return run

    return ([norm] + [hidden_block(c0) for c0 in range(0, hidden, MXU_COLS)]
            + [out_block(c0) for c0 in range(0, x_ref.shape[1], MXU_COLS)])


def _ffn_qkv_kernel(x_ref, norm_ref, w_in_ref, w_out_ref, *rest, rows, hidden):
    qkv_refs, (out_ref, qt_ref, k_ref, vt_ref, xprev_ref, act_ref, qall_ref, kvt_ref) = (
        rest[:-8], rest[-8:])

    @pl.when(pl.program_id(0) == 0)
    def _():
        xprev_ref[...] = jnp.zeros_like(xprev_ref)

    qkv = _qkv_steps((xprev_ref,) + tuple(qkv_refs) + (qt_ref, k_ref, vt_ref, qall_ref, kvt_ref),
                     rows, True)
    ffn = _ffn_steps(x_ref, norm_ref, w_in_ref, w_out_ref, (out_ref, xprev_ref), act_ref, hidden)
    _interleave(qkv, ffn)


def _ffn(x, norm, w_in, w_out, attn_t=None, w_o=None):
    s, d = x.shape
    hidden = w_out.shape[0]
    rows = FFN_ROWS
    row_spec = pl.BlockSpec((rows, d), lambda i: (i, 0))
    ffn_specs = [_full((1, d)), _full((d, 2 * hidden)), _full((hidden, d))]
    ffn_args = (norm.reshape(1, d), w_in.astype(BF16), w_out.astype(BF16))
    if attn_t is None:
        kern = functools.partial(_ffn_kernel, hidden=hidden)
        in_specs = [row_spec] + ffn_specs
        args = (x,) + ffn_args
        name = "ffn"
    else:
        kern = functools.partial(_proj_ffn_kernel, hidden=hidden)
        in_specs = [row_spec, pl.BlockSpec((attn_t.shape[0], rows), lambda i: (0, i)),
                    _full(w_o.shape)] + ffn_specs
        args = (x, attn_t, w_o.astype(BF16)) + ffn_args
        name = "attn_proj_ffn"
    return pl.pallas_call(
        kern,
        out_shape=jax.ShapeDtypeStruct((s, d), F32),
        grid=(s // rows,),
        in_specs=in_specs,
        out_specs=row_spec,
        compiler_params=_params(("parallel",)),
        name=name,
    )(*args)


def _qkv_steps(refs, rows, with_kv):
    if with_kv:
        (x_ref, pos_ref, inv_freq_ref, qn_ref, w_dq_ref, qln_ref, w_uqt_ref, qg_ref,
         kvn_ref, w_dc_ref, w_drt_ref, lnorm_ref, w_upt_ref, kg_ref,
         qt_ref, k_ref, vt_ref, qall_ref, kvt_ref) = refs
    else:
        (x_ref, pos_ref, inv_freq_ref, qn_ref, w_dq_ref, qln_ref, w_uqt_ref, qg_ref,
         qt_ref, qall_ref) = refs
    half = MLA_ROPE // 2
    group = 4
    pad = jnp.zeros((QK_PAD - MLA_QK, rows), F32)
    cell = {}

    def prologue():
        x = x_ref[...]
        cell["xn"] = x * lax.rsqrt(jnp.mean(x * x, axis=-1, keepdims=True) + EPS)
        ang = pos_ref[...].astype(F32) * inv_freq_ref[...]
        cell["cos"], cell["sin"] = jnp.cos(ang), jnp.sin(ang)

    def q_latent():
        h = (cell["xn"] * qn_ref[...]).astype(BF16)
        cell["c_q"] = _rms_rows(_dot(h, w_dq_ref[...]), qln_ref[...]).astype(BF16)

    def q_heads(h0):
        def run():
            r0, r1 = h0 * MLA_QK, (h0 + group) * MLA_QK
            qall_ref[r0:r1, :] = _dot_nt(w_uqt_ref[r0:r1, :], cell["c_q"])
            g = qg_ref[...]
            cos, sin = cell["cos"], cell["sin"]
            scale = MLA_QK ** -0.5 * LOG2_E
            for hh in range(h0, h0 + group):
                q = qall_ref[hh * MLA_QK:(hh + 1) * MLA_QK, :]
                inv = lax.rsqrt(jnp.mean(q * q, axis=0, keepdims=True) + EPS)
                qn = q * inv * g
                x1, x2 = qn[MLA_NOPE:MLA_NOPE + half, :], qn[MLA_NOPE + half:, :]
                qt = jnp.concatenate([qn[:MLA_NOPE, :], x1 * cos - x2 * sin,
                                      x1 * sin + x2 * cos, pad], axis=0)
                qt_ref[hh] = (qt * scale).astype(BF16)
        return run

    q_steps = [q_latent] + [q_heads(h0) for h0 in range(0, MLA_HEADS, group)]
    if not with_kv:
        return [prologue] + q_steps

    per_head = MLA_NOPE + MLA_V

    def kv_latent():
        h = (cell["xn"] * kvn_ref[...]).astype(BF16)
        cos, sin = cell["cos"], cell["sin"]
        cell["c_kv"] = _rms_rows(_dot(h, w_dc_ref[...]), lnorm_ref[...]).astype(BF16)
        kr = _dot_nt(w_drt_ref[...], h)
        cell["ss_rope"] = jnp.sum(kr * kr, axis=0, keepdims=True)
        krg = kr * kg_ref[MLA_NOPE:, :]
        x1, x2 = krg[:half, :], krg[half:, :]
        cell["r1"] = x1 * cos - x2 * sin
        cell["r2"] = x1 * sin + x2 * cos

    def kv_heads(h0):
        def run():
            r0, r1 = h0 * per_head, (h0 + group) * per_head
            kvt_ref[r0:r1, :] = _dot_nt(w_upt_ref[r0:r1, :], cell["c_kv"])
            g_nope = kg_ref[:MLA_NOPE, :]
            for hh in range(h0, h0 + group):
                kn = kvt_ref[hh * per_head:hh * per_head + MLA_NOPE, :]
                ss = jnp.sum(kn * kn, axis=0, keepdims=True) + cell["ss_rope"]
                inv = lax.rsqrt(ss * (1.0 / MLA_QK) + EPS)
                kt = jnp.concatenate([kn * inv * g_nope, cell["r1"] * inv, cell["r2"] * inv, pad],
                                     axis=0)
                k_ref[hh] = kt.T.astype(BF16)
                vt_ref[hh, 0] = kvt_ref[hh * per_head + MLA_NOPE:(hh + 1) * per_head, :].astype(BF16)
        return run

    kv_steps = [kv_latent] + [kv_heads(h0) for h0 in range(0, MLA_HEADS, group)]
    return [prologue] + _alternate(kv_steps, q_steps)


def _qkv_kernel(*refs, rows, with_kv):
    _interleave(_qkv_steps(refs, rows, with_kv), [])


def _qkv_operands(s, d, rows, tile, pos, inv_freq, q_params, kv_params):
    q_pre, w_dq, q_lat, w_uq, q_gain = q_params
    w_uqt = w_uq.T.astype(BF16)
    in_specs = [pl.BlockSpec((1, rows), lambda i: (0, tile(i))), _full((MLA_ROPE // 2, 1)),
                _full((1, d)), _full(w_dq.shape), _full((1, w_dq.shape[1])), _full(w_uqt.shape),
                _full((MLA_QK, 1))]
    args = [pos, inv_freq, q_pre.reshape(1, d), w_dq.astype(BF16), q_lat.reshape(1, -1), w_uqt,
            q_gain.reshape(-1, 1)]
    out_shape = [jax.ShapeDtypeStruct((MLA_HEADS, QK_PAD, s), BF16)]
    out_specs = [pl.BlockSpec((MLA_HEADS, QK_PAD, rows), lambda i: (0, 0, tile(i)))]
    scratch = [pltpu.VMEM((w_uqt.shape[0], rows), F32)]
    if kv_params is not None:
        kv_pre, w_down, kv_lat, w_up, k_gain = kv_params
        w_dc = w_down[:, :MLA_KV_RANK].astype(BF16)
        w_drt = w_down[:, MLA_KV_RANK:].T.astype(BF16)
        w_upt = w_up.T.astype(BF16)
        in_specs += [_full((1, d)), _full(w_dc.shape), _full(w_drt.shape), _full((1, MLA_KV_RANK)),
                     _full(w_upt.shape), _full((MLA_QK, 1))]
        args += [kv_pre.reshape(1, d), w_dc, w_drt, kv_lat.reshape(1, -1), w_upt,
                 k_gain.reshape(-1, 1)]
        out_shape += [jax.ShapeDtypeStruct((MLA_HEADS, s, QK_PAD), BF16),
                      jax.ShapeDtypeStruct((MLA_HEADS, s // rows, MLA_V, rows), BF16)]
        out_specs += [pl.BlockSpec((MLA_HEADS, rows, QK_PAD), lambda i: (0, tile(i), 0)),
                      pl.BlockSpec((MLA_HEADS, 1, MLA_V, rows), lambda i: (0, tile(i), 0, 0))]
        scratch += [pltpu.VMEM((w_upt.shape[0], rows), F32)]
    return in_specs, args, out_shape, out_specs, scratch


def _mla_qkv(x, pos, inv_freq, q_params, kv_params=None):
    s, d = x.shape
    rows = PROJ_ROWS
    in_specs, args, out_shape, out_specs, scratch = _qkv_operands(
        s, d, rows, lambda i: i, pos, inv_freq, q_params, kv_params)
    kern = functools.partial(_qkv_kernel, rows=rows, with_kv=kv_params is not None)
    return pl.pallas_call(
        kern,
        out_shape=tuple(out_shape),
        grid=(s // rows,),
        in_specs=[pl.BlockSpec((rows, d), lambda i: (i, 0))] + in_specs,
        out_specs=tuple(out_specs),
        scratch_shapes=scratch,
        compiler_params=_params(("parallel",)),
        name="mla_qkv" if kv_params is not None else "mla_q",
    )(x, *args)


def _ffn_qkv(x, norm, w_in, w_out, pos, inv_freq, q_params, kv_params):
    s, d = x.shape
    hidden = w_out.shape[0]
    rows = PROJ_ROWS
    n_tiles = s // rows
    in_specs, args, out_shape, out_specs, scratch = _qkv_operands(
        s, d, rows, lambda i: jnp.maximum(i - 1, 0), pos, inv_freq, q_params, kv_params)
    row_spec = pl.BlockSpec((rows, d), lambda i: (jnp.minimum(i, n_tiles - 1), 0))
    kern = functools.partial(_ffn_qkv_kernel, rows=rows, hidden=hidden)
    return pl.pallas_call(
        kern,
        out_shape=(jax.ShapeDtypeStruct((s, d), F32),) + tuple(out_shape),
        grid=(n_tiles + 1,),
        in_specs=[row_spec, _full((1, d)), _full((d, 2 * hidden)), _full((hidden, d))] + in_specs,
        out_specs=(row_spec,) + tuple(out_specs),
        scratch_shapes=[pltpu.VMEM((rows, d), F32), pltpu.VMEM((rows, hidden), BF16)] + scratch,
        compiler_params=_params(("arbitrary",)),
        name="ffn_qkv",
    )(x, norm.reshape(1, d), w_in.astype(BF16), w_out.astype(BF16), *args)


def _attn_kernel(bounded_ref, qt_ref, k_ref, vt_ref, o_ref, acc_ref, l_ref, p0_ref, p1_ref, *,
                 tq, tk):
    i = pl.program_id(1)
    heads = qt_ref.shape[0]
    per_q = tq // tk
    assert per_q == 2
    key_c = lax.broadcasted_iota(jnp.int32, (tk, tk), 0) >> 6
    qry_c = lax.broadcasted_iota(jnp.int32, (tk, tk), 1) >> 6
    diag_mask = key_c <= qry_c

    def k_rows(hd, start, size):
        return k_ref[hd, pl.ds(pl.multiple_of(start, size), size), :]

    @pl.when(bounded_ref[0] != 0)
    def _():
        qts = [qt_ref[hd] for hd in range(heads)]

        def sublane_sums(p):
            return p.reshape(p.shape[0] // SUBLANES, SUBLANES, p.shape[1]).sum(axis=0)

        def weighted_values(hd, p_ref, j):
            return _dot(vt_ref[hd, j], p_ref[hd])

        j0, j1 = 2 * i, 2 * i + 1
        for hd in range(heads):
            p = jnp.where(diag_mask, jnp.exp2(_dot(k_rows(hd, j1 * tk, tk), qts[hd][:, tk:])), 0.0)
            p1_ref[hd] = jnp.concatenate([jnp.zeros((tk, tk), F32), p], axis=1).astype(BF16)
            l_ref[hd] = jnp.concatenate([jnp.zeros((SUBLANES, tk), F32), sublane_sums(p)], axis=1)
            acc_ref[hd] = jnp.zeros((MLA_V, tq), F32)

        def tile_pairs(t, n_pairs):
            half = tk // 2
            tot = [None] * heads
            l_tot = [None] * heads
            for u in range(n_pairs):
                prev = 2 * (t + u) - 1
                if u == 0:
                    prev = jnp.where(t == 0, j1, prev)
                stages = ((p0_ref, 2 * (t + u), p1_ref, prev),
                          (p1_ref, 2 * (t + u) + 1, p0_ref, 2 * (t + u)))
                for new_ref, j_new, old_ref, j_old in stages:
                    for hd in range(heads):
                        p_new = jnp.exp2(_dot(k_rows(hd, j_new * tk, tk), qts[hd]))
                        l_part = sublane_sums(p_new)
                        l_tot[hd] = l_part if l_tot[hd] is None else l_tot[hd] + l_part
                        for r0 in (0, half):
                            part = _dot(vt_ref[hd, j_old][:, r0:r0 + half],
                                        old_ref[hd, r0:r0 + half, :])
                            tot[hd] = part if tot[hd] is None else tot[hd] + part
                        new_ref[hd] = p_new.astype(BF16)
            for hd in range(heads):
                acc_ref[hd] += tot[hd]
                l_ref[hd] += l_tot[hd]

        done = 0
        for n_pairs, trips in ((4, i >> 2), (2, (i >> 1) & 1), (1, i & 1)):
            def body(u, carry, n_pairs=n_pairs, done=done):
                tile_pairs(done + n_pairs * u, n_pairs)
                return carry

            lax.fori_loop(0, trips, body, 0)
            done = done + n_pairs * trips

        j_last = jnp.where(i == 0, j1, 2 * i - 1)
        for hd in range(heads):
            p = jnp.exp2(_dot(k_rows(hd, j0 * tk, tk), qts[hd]))
            p = jnp.concatenate([jnp.where(diag_mask, p[:, :tk], 0.0), p[:, tk:]], axis=1)
            acc = (acc_ref[hd] + weighted_values(hd, p1_ref, j_last)
                   + _dot(vt_ref[hd, j0], p.astype(BF16)))
            l = jnp.sum(l_ref[hd] + sublane_sums(p), axis=0, keepdims=True)
            o_ref[hd * MLA_V:(hd + 1) * MLA_V, :] = (acc / l).astype(BF16)

    @pl.when(bounded_ref[0] == 0)
    def _():
        key_w = lax.broadcasted_iota(jnp.int32, (tk, tq), 0) >> 6
        qry_w = lax.broadcasted_iota(jnp.int32, (tk, tq), 1) >> 6
        for hd in range(heads):
            qt = qt_ref[hd]

            def step(j, carry, mask, hd=hd, qt=qt):
                m, l, acc = carry
                s = _dot(k_rows(hd, j * tk, tk), qt)
                if mask is not None:
                    s = jnp.where(mask, s, -jnp.inf)
                m_new = jnp.maximum(m, jnp.max(s, axis=0, keepdims=True))
                alpha = jnp.exp2(m - m_new)
                p = jnp.exp2(s - m_new)
                l = alpha * l + jnp.sum(p, axis=0, keepdims=True)
                acc = alpha * acc + _dot(vt_ref[hd, j], p.astype(BF16))
                return m_new, l, acc

            carry = (jnp.full((1, tq), -jnp.inf, F32), jnp.zeros((1, tq), F32),
                     jnp.zeros((MLA_V, tq), F32))
            carry = lax.fori_loop(0, i * per_q, lambda j, c, step=step: step(j, c, None), carry)
            for jj in range(per_q):
                carry = step(i * per_q + jj, carry, (key_w + jj * (tk // CHUNK)) <= qry_w)
            _, l, acc = carry
            o_ref[hd * MLA_V:(hd + 1) * MLA_V, :] = (acc / l).astype(BF16)


def _mla_attention(bounded, qt, k, vt):
    heads, _, s = qt.shape
    tq, tk, hb = ATTN_TQ, ATTN_TK, ATTN_HEADS_PER_STEP
    kern = functools.partial(_attn_kernel, tq=tq, tk=tk)
    return pl.pallas_call(
        kern,
        out_shape=jax.ShapeDtypeStruct((heads * MLA_V, s), BF16),
        grid_spec=pltpu.PrefetchScalarGridSpec(
            num_scalar_prefetch=1,
            grid=(heads // hb, s // tq),
            in_specs=[
                pl.BlockSpec((hb, QK_PAD, tq), lambda h, i, b: (h, 0, i)),
                pl.BlockSpec((hb, s, QK_PAD), lambda h, i, b: (h, 0, 0)),
                pl.BlockSpec((hb, s // tk, MLA_V, tk), lambda h, i, b: (h, 0, 0, 0)),
            ],
            out_specs=pl.BlockSpec((hb * MLA_V, tq), lambda h, i, b: (h, i)),
            scratch_shapes=[pltpu.VMEM((hb, MLA_V, tq), F32), pltpu.VMEM((hb, SUBLANES, tq), F32),
                            pltpu.VMEM((hb, tk, tq), BF16), pltpu.VMEM((hb, tk, tq), BF16)],
        ),
        compiler_params=_params(("parallel", "arbitrary")),
        name="mla_attn",
    )(bounded, qt, k, vt)


def kernel(x, positions, a_norm, a_w_in, a_w_gate_up, a_b_gate, a_out_norm, a_w_out, b_norm, b_w_dq,
           b_q_latent_norm, b_w_uq, b_q_norm, b_w_out, kv_norm, kv_w_down, kv_latent_norm, kv_w_up,
           k_norm, f_norm, f_w_in, f_w_out):
    batch, s, d = x.shape
    n_a = a_norm.shape[0]
    n_b = b_norm.shape[0]
    half = MLA_ROPE // 2
    inv_freq = (ROPE_THETA ** (-jnp.arange(half, dtype=F32) / half)).reshape(half, 1)
    kv_params = (kv_norm, kv_w_down, kv_latent_norm, kv_w_up, k_norm)

    def q_params(j):
        return b_norm[j], b_w_dq[j], b_q_latent_norm[j], b_w_uq[j], b_q_norm[j]

    outs = []
    for bi in range(batch):
        xb = x[bi]
        pos = positions[bi].reshape(1, s)
        qt_first = k_sh = vt_sh = None
        for layer in range(n_a + n_b):
            if layer < n_a:
                xb = _gla_layer(xb, a_norm[layer], a_w_in[layer], a_w_gate_up[layer],
                                a_b_gate[layer], a_out_norm[layer], a_w_out[layer])
                if layer == n_a - 1 and n_b > 0:
                    xb, qt_first, k_sh, vt_sh = _ffn_qkv(
                        xb, f_norm[layer], f_w_in[layer], f_w_out[layer], pos, inv_freq,
                        q_params(0), kv_params)
                else:
                    xb = _ffn(xb, f_norm[layer], f_w_in[layer], f_w_out[layer])
            else:
                j = layer - n_a
                qt = qt_first if j == 0 else _mla_qkv(xb, pos, inv_freq, q_params(j))[0]
                logit_bound = (LOGIT_BOUND_COEF * jnp.max(jnp.abs(b_q_norm[j]))
                               * jnp.max(jnp.abs(k_norm)))
                bounded = (logit_bound <= MAX_BOUNDED_LOGIT).astype(jnp.int32).reshape(1)
                attn_t = _mla_attention(bounded, qt, k_sh, vt_sh)
                xb = _ffn(xb, f_norm[layer], f_w_in[layer], f_w_out[layer],
                          attn_t=attn_t, w_o=b_w_out[j])
        outs.append(xb)
    return jnp.stack(outs, axis=0)
```

```python
import functools
import itertools

import jax
import jax.numpy as jnp
from jax import lax
from jax.experimental import pallas as pl
from jax.experimental.pallas import tpu as pltpu

F32 = jnp.float32
BF16 = jnp.bfloat16

EPS = 1e-6
CHUNK = 64
GLA_HEADS = 4
GLA_TAU = 16.0
GLA_GATE_RANK = 16
MLA_HEADS = 16
MLA_NOPE = 64
MLA_ROPE = 32
MLA_QK = MLA_NOPE + MLA_ROPE
MLA_V = 64
MLA_KV_RANK = 256
ROPE_THETA = 10000.0
LOG2_E = 1.4426950408889634

MXU_COLS = 256
LANES = 128
SUBLANES = 8
QK_PAD = LANES
LOGIT_BOUND_COEF = 1.02 * LOG2_E * MLA_QK ** 0.5
MAX_BOUNDED_LOGIT = 100.0
V7X_VMEM_LIMIT_BYTES = 56 * 1024 * 1024

GLA_SUB_ROWS = 256
GLA_ROWS = 2 * GLA_SUB_ROWS
FFN_ROWS = 512
CAST_ROWS = 256
ATTN_TQ = 1024
ATTN_TK = ATTN_TQ // 2
ATTN_HEADS_PER_STEP = 2
PROJ_ROWS = ATTN_TK


def _dot(a, b):
    return jnp.dot(a, b, preferred_element_type=F32)


def _dot_nt(a, b):
    return lax.dot_general(a, b, (((1,), (1,)), ((), ())), preferred_element_type=F32)


def _dot_tn(a, b):
    return lax.dot_general(a, b, (((0,), (0,)), ((), ())), preferred_element_type=F32)


def _rms_rows(x, g):
    return x * lax.rsqrt(jnp.mean(x * x, axis=-1, keepdims=True) + EPS) * g


def _silu(x):
    return x * jax.nn.sigmoid(x)


def _params(semantics):
    return pltpu.CompilerParams(dimension_semantics=semantics,
                                vmem_limit_bytes=V7X_VMEM_LIMIT_BYTES)


def _alternate(first, second):
    pairs = itertools.zip_longest(first, second)
    return [step for pair in pairs for step in pair if step is not None]


def _interleave(first, second):
    for step in _alternate(first, second):
        step()


def _full(shape):
    return pl.BlockSpec(shape, lambda *_: (0,) * len(shape), pipeline_mode=pl.Buffered(1))


def _layer_block(stack, layer):
    shape = stack.shape[1:]
    return pl.BlockSpec((None,) + shape, lambda *_: (layer,) + (0,) * len(shape),
                        pipeline_mode=pl.Buffered(1))


def _cast_kernel(w_ref, o_ref):
    o_ref[...] = w_ref[...].astype(o_ref.dtype)


def _to_bf16(stack):
    layers, rows, cols = stack.shape
    block = pl.BlockSpec((None, CAST_ROWS, cols), lambda l, i: (l, i, 0))
    return pl.pallas_call(
        _cast_kernel,
        out_shape=jax.ShapeDtypeStruct(stack.shape, BF16),
        grid=(layers, rows // CAST_ROWS),
        in_specs=[block],
        out_specs=block,
        compiler_params=_params(("parallel", "parallel")),
        name="to_bf16",
    )(stack)


def _gla_kernel(x_ref, xn_ref, norm_ref, w_main_ref, w_gl_ref, w_gate_ref, b_gate_ref, onorm_ref,
                w_out_ref, out_ref, state_ref, *scratch, sub, dk, dv):
    hk, hv = dk // GLA_HEADS, dv // GLA_HEADS
    subs = [scratch[0:3], scratch[3:6]]

    @pl.when(pl.program_id(0) == 0)
    def _():
        state_ref[...] = jnp.zeros_like(state_ref)

    r_i = lax.broadcasted_iota(jnp.int32, (sub, sub), 0)
    c_i = lax.broadcasted_iota(jnp.int32, (sub, sub), 1)
    tri = ((c_i <= r_i) & ((r_i >> 6) == (c_i >> 6))).astype(BF16)
    qi = lax.broadcasted_iota(jnp.int32, (CHUNK, CHUNK), 0)
    kj = lax.broadcasted_iota(jnp.int32, (CHUNK, CHUNK), 1)
    causal = kj <= qi
    q_scale = hk ** -0.5

    def project_pieces(x_rows, a):
        proj_ref, b_ref, _ = subs[a]
        cell = {}

        def norm():
            cell["h"] = _rms_rows(x_rows[...], norm_ref[...]).astype(BF16)

        def main_block(c0):
            def run():
                proj_ref[:, c0:c0 + MXU_COLS] = _dot(cell["h"], w_main_ref[:, c0:c0 + MXU_COLS])
            return run

        def gate_logits():
            gl = _dot(cell["h"], w_gl_ref[...]).astype(BF16)
            z = _dot(gl, w_gate_ref[...]) + b_gate_ref[...]
            cell["log_a"] = ((jnp.minimum(z, 0.0) - jnp.log1p(jnp.exp(-jnp.abs(z))))
                             * (1.0 / GLA_TAU))

        def cumulate():
            log_a = cell["log_a"]
            hi = log_a.astype(BF16)
            rest = log_a - hi.astype(F32)
            mid = rest.astype(BF16)
            lo = (rest - mid.astype(F32)).astype(BF16)
            b_ref[...] = _dot(tri, hi) + _dot(tri, mid) + _dot(tri, lo)

        blocks = [main_block(c0) for c0 in range(0, 2 * dk + 2 * dv, MXU_COLS)]
        return [norm, gate_logits] + blocks[:2] + [cumulate] + blocks[2:]

    def recur_pieces(a):
        proj_ref, b_ref, o_ref = subs[a]
        cell = {}

        def prepare(c):
            r0 = c * CHUNK
            b_c = b_ref[r0:r0 + CHUNK, :]
            b_last = b_c[CHUNK - 1:CHUNK, :]
            q_c = proj_ref[r0:r0 + CHUNK, 0:dk]
            k_c = proj_ref[r0:r0 + CHUNK, dk:2 * dk]
            cell[c] = dict(
                v=proj_ref[r0:r0 + CHUNK, 2 * dk:2 * dk + dv].astype(BF16),
                q_dec=(q_c * q_scale * jnp.exp(b_c)).astype(BF16),
                k_inv=(k_c * jnp.exp(-b_c)).astype(BF16),
                k_end=(k_c * jnp.exp(b_last - b_c)).astype(BF16),
                decay=jnp.exp(b_last))

        def unit(c, hh):
            def run():
                if hh == 0:
                    prepare(c)
                t = cell[c]
                r0 = c * CHUNK
                ks = slice(hh * hk, (hh + 1) * hk)
                vs = slice(hh * hv, (hh + 1) * hv)
                attn = jnp.where(causal, _dot_nt(t["q_dec"][:, ks], t["k_inv"][:, ks]),
                                 0.0).astype(BF16)
                state_t = state_ref[hh]
                o_h = (_dot(attn, t["v"][:, vs])
                       + _dot_nt(t["q_dec"][:, ks], state_t.astype(BF16)))
                state_ref[hh] = (t["decay"][:, ks] * state_t
                                 + _dot_tn(t["v"][:, vs], t["k_end"][:, ks]))
                o_ref[r0:r0 + CHUNK, vs] = _rms_rows(o_h, onorm_ref[...])
            return run

        return [unit(c, hh) for c in range(sub // CHUNK) for hh in range(GLA_HEADS)]

    def output_pieces(a):
        proj_ref, _, o_ref = subs[a]
        rows = slice(a * sub, (a + 1) * sub)
        cell = {}

        def gate():
            g = _silu(proj_ref[:, 2 * dk + dv:2 * dk + 2 * dv])
            cell["g"] = (o_ref[...] * g).astype(BF16)

        def out_block(c0):
            def run():
                cols = slice(c0, c0 + MXU_COLS)
                out_ref[rows, cols] = x_ref[rows, cols] + _dot(cell["g"], w_out_ref[:, cols])
            return run

        return [gate] + [out_block(c0) for c0 in range(0, out_ref.shape[1], MXU_COLS)]

    @pl.when(pl.program_id(0) == 0)
    def _():
        _interleave(project_pieces(x_ref.at[0:sub, :], 0), [])

    next_proj = project_pieces(xn_ref, 0)
    out0, out1 = output_pieces(0), output_pieces(1)
    split = len(next_proj) - len(out1) - 1
    _interleave(recur_pieces(0), project_pieces(x_ref.at[sub:2 * sub, :], 1))
    _interleave(recur_pieces(1), out0 + next_proj[:split])
    _interleave(out1, next_proj[split:])


def _gla_layer(x, norm, w_in, w_gate_up, b_gate, out_norm, w_out):
    s, d = x.shape
    dk = w_gate_up.shape[1]
    dv = (w_in.shape[1] - 2 * dk - GLA_GATE_RANK) // 2
    n_main = 2 * dk + 2 * dv
    rows = GLA_ROWS
    w_main = w_in[:, :n_main].astype(BF16)
    w_gl = jnp.pad(w_in[:, n_main:], ((0, 0), (0, LANES - GLA_GATE_RANK))).astype(BF16)
    w_gate = jnp.pad(w_gate_up, ((0, LANES - GLA_GATE_RANK), (0, 0))).astype(BF16)
    sub = GLA_SUB_ROWS
    last_sub = s // sub - 1
    sub_scratch = [pltpu.VMEM((sub, n_main), F32), pltpu.VMEM((sub, dk), F32),
                   pltpu.VMEM((sub, dv), F32)]
    kern = functools.partial(_gla_kernel, sub=sub, dk=dk, dv=dv)
    return pl.pallas_call(
        kern,
        out_shape=jax.ShapeDtypeStruct((s, d), F32),
        grid=(s // rows,),
        in_specs=[
            pl.BlockSpec((rows, d), lambda i: (i, 0)),
            pl.BlockSpec((sub, d), lambda i: (jnp.minimum(2 * i + 2, last_sub), 0)),
            _full((1, d)),
            _full((d, n_main)),
            _full((d, LANES)),
            _full((LANES, dk)),
            _full((1, dk)),
            _full((1, dv // GLA_HEADS)),
            _full((dv, d)),
        ],
        out_specs=pl.BlockSpec((rows, d), lambda i: (i, 0)),
        scratch_shapes=[pltpu.VMEM((GLA_HEADS, dv // GLA_HEADS, dk // GLA_HEADS), F32)]
        + sub_scratch * 2,
        compiler_params=_params(("arbitrary",)),
        name="gla_layer",
    )(x, x, norm.reshape(1, d), w_main, w_gl, w_gate, b_gate.reshape(1, dk),
      out_norm.reshape(1, -1), w_out.astype(BF16))


def _ffn_body(x, norm_ref, w_in_ref, w_out_ref, out_ref, hidden):
    h = _rms_rows(x, norm_ref[...]).astype(BF16)
    gate = _dot(h, w_in_ref[:, 0:hidden])
    up = _dot(h, w_in_ref[:, hidden:2 * hidden])
    act = (_silu(gate) * up).astype(BF16)
    out_ref[...] = x + _dot(act, w_out_ref[...])


def _ffn_kernel(x_ref, norm_ref, w_in_ref, w_out_ref, out_ref, *, hidden):
    _ffn_body(x_ref[...], norm_ref, w_in_ref, w_out_ref, out_ref, hidden)


def _proj_ffn_kernel(x_ref, ot_ref, w_o_ref, norm_ref, w_in_ref, w_out_ref, out_ref, *, hidden):
    x = x_ref[...] + _dot_tn(ot_ref[...], w_o_ref[...])
    _ffn_body(x, norm_ref, w_in_ref, w_out_ref, out_ref, hidden)


def _ffn_steps(x_ref, norm_ref, w_in_ref, w_out_ref, out_refs, act_ref, hidden):
    cell = {}

    def norm():
        cell["h"] = _rms_rows(x_ref[...], norm_ref[...]).astype(BF16)

    def hidden_block(c0):
        def run():
            gate = _dot(cell["h"], w_in_ref[:, c0:c0 + MXU_COLS])
            up = _dot(cell["h"], w_in_ref[:, hidden + c0:hidden + c0 + MXU_COLS])
            act_ref[:, c0:c0 + MXU_COLS] = (_silu(gate) * up).astype(BF16)
        return run

    def out_block(c0):
        def run():
            cols = slice(c0, c0 + MXU_COLS)
            y = x_ref[:, cols] + _dot(act_ref[...], w_out_ref[:, cols])
            for ref in out_refs:
                ref[:, cols] = y
        return run

    return ([norm] + [hidden_block(c0) for c0 in range(0, hidden, MXU_COLS)]
            + [out_block(c0) for c0 in range(0, x_ref.shape[1], MXU_COLS)])


def _ffn_qkv_kernel(x_ref, norm_ref, w_in_ref, w_out_ref, *rest, rows, hidden):
    qkv_refs, (out_ref, qt_ref, k_ref, vt_ref, xprev_ref, act_ref, qall_ref, kvt_ref) = (
        rest[:-8], rest[-8:])

    @pl.when(pl.program_id(0) == 0)
    def _():
        xprev_ref[...] = jnp.zeros_like(xprev_ref)

    qkv = _qkv_steps((xprev_ref,) + tuple(qkv_refs) + (qt_ref, k_ref, vt_ref, qall_ref, kvt_ref),
                     rows, True)
    ffn = _ffn_steps(x_ref, norm_ref, w_in_ref, w_out_ref, (out_ref, xprev_ref), act_ref, hidden)
    _interleave(qkv, ffn)


def _ffn(x, norm, w_in, w_out, layer, attn_t=None, w_o=None):
    s, d = x.shape
    hidden = w_out.shape[1]
    rows = FFN_ROWS
    row_spec = pl.BlockSpec((rows, d), lambda i: (i, 0))
    ffn_specs = [_full((1, d)), _layer_block(w_in, layer), _layer_block(w_out, layer)]
    ffn_args = (norm.reshape(1, d), w_in, w_out)
    if attn_t is None:
        kern = functools.partial(_ffn_kernel, hidden=hidden)
        in_specs = [row_spec] + ffn_specs
        args = (x,) + ffn_args
        name = "ffn"
    else:
        kern = functools.partial(_proj_ffn_kernel, hidden=hidden)
        in_specs = [row_spec, pl.BlockSpec((attn_t.shape[0], rows), lambda i: (0, i)),
                    _full(w_o.shape)] + ffn_specs
        args = (x, attn_t, w_o.astype(BF16)) + ffn_args
        name = "attn_proj_ffn"
    return pl.pallas_call(
        kern,
        out_shape=jax.ShapeDtypeStruct((s, d), F32),
        grid=(s // rows,),
        in_specs=in_specs,
        out_specs=row_spec,
        compiler_params=_params(("parallel",)),
        name=name,
    )(*args)


def _qkv_steps(refs, rows, with_kv):
    if with_kv:
        (x_ref, pos_ref, inv_freq_ref, qn_ref, w_dq_ref, qln_ref, w_uqt_ref, qg_ref,
         kvn_ref, w_dc_ref, w_drt_ref, lnorm_ref, w_upt_ref, kg_ref,
         qt_ref, k_ref, vt_ref, qall_ref, kvt_ref) = refs
    else:
        (x_ref, pos_ref, inv_freq_ref, qn_ref, w_dq_ref, qln_ref, w_uqt_ref, qg_ref,
         qt_ref, qall_ref) = refs
    half = MLA_ROPE // 2
    group = 4
    pad = jnp.zeros((QK_PAD - MLA_QK, rows), F32)
    cell = {}

    def prologue():
        x = x_ref[...]
        cell["xn"] = x * lax.rsqrt(jnp.mean(x * x, axis=-1, keepdims=True) + EPS)
        ang = pos_ref[...].astype(F32) * inv_freq_ref[...]
        cell["cos"], cell["sin"] = jnp.cos(ang), jnp.sin(ang)

    def q_latent():
        h = (cell["xn"] * qn_ref[...]).astype(BF16)
        cell["c_q"] = _rms_rows(_dot(h, w_dq_ref[...]), qln_ref[...]).astype(BF16)

    def q_heads(h0):
        def run():
            r0, r1 = h0 * MLA_QK, (h0 + group) * MLA_QK
            qall_ref[r0:r1, :] = _dot_nt(w_uqt_ref[r0:r1, :], cell["c_q"])
            g = qg_ref[...]
            cos, sin = cell["cos"], cell["sin"]
            scale = MLA_QK ** -0.5 * LOG2_E
            for hh in range(h0, h0 + group):
                q = qall_ref[hh * MLA_QK:(hh + 1) * MLA_QK, :]
                inv = lax.rsqrt(jnp.mean(q * q, axis=0, keepdims=True) + EPS)
                qn = q * inv * g
                x1, x2 = qn[MLA_NOPE:MLA_NOPE + half, :], qn[MLA_NOPE + half:, :]
                qt = jnp.concatenate([qn[:MLA_NOPE, :], x1 * cos - x2 * sin,
                                      x1 * sin + x2 * cos, pad], axis=0)
                qt_ref[hh] = (qt * scale).astype(BF16)
        return run

    q_steps = [q_latent] + [q_heads(h0) for h0 in range(0, MLA_HEADS, group)]
    if not with_kv:
        return [prologue] + q_steps

    per_head = MLA_NOPE + MLA_V

    def kv_latent():
        h = (cell["xn"] * kvn_ref[...]).astype(BF16)
        cos, sin = cell["cos"], cell["sin"]
        cell["c_kv"] = _rms_rows(_dot(h, w_dc_ref[...]), lnorm_ref[...]).astype(BF16)
        kr = _dot_nt(w_drt_ref[...], h)
        cell["ss_rope"] = jnp.sum(kr * kr, axis=0, keepdims=True)
        krg = kr * kg_ref[MLA_NOPE:, :]
        x1, x2 = krg[:half, :], krg[half:, :]
        cell["r1"] = x1 * cos - x2 * sin
        cell["r2"] = x1 * sin + x2 * cos

    def kv_heads(h0):
        def run():
            r0, r1 = h0 * per_head, (h0 + group) * per_head
            kvt_ref[r0:r1, :] = _dot_nt(w_upt_ref[r0:r1, :], cell["c_kv"])
            g_nope = kg_ref[:MLA_NOPE, :]
            for hh in range(h0, h0 + group):
                kn = kvt_ref[hh * per_head:hh * per_head + MLA_NOPE, :]
                ss = jnp.sum(kn * kn, axis=0, keepdims=True) + cell["ss_rope"]
                inv = lax.rsqrt(ss * (1.0 / MLA_QK) + EPS)
                kt = jnp.concatenate([kn * inv * g_nope, cell["r1"] * inv, cell["r2"] * inv, pad],
                                     axis=0)
                k_ref[hh] = kt.T.astype(BF16)
                vt_ref[hh, 0] = kvt_ref[hh * per_head + MLA_NOPE:(hh + 1) * per_head, :].astype(BF16)
        return run

    kv_steps = [kv_latent] + [kv_heads(h0) for h0 in range(0, MLA_HEADS, group)]
    return [prologue] + _alternate(kv_steps, q_steps)


def _qkv_kernel(*refs, rows, with_kv):
    _interleave(_qkv_steps(refs, rows, with_kv), [])


def _qkv_operands(s, d, rows, tile, pos, inv_freq, q_params, kv_params):
    q_pre, w_dq, q_lat, w_uq, q_gain = q_params
    w_uqt = w_uq.T.astype(BF16)
    in_specs = [pl.BlockSpec((1, rows), lambda i: (0, tile(i))), _full((MLA_ROPE // 2, 1)),
                _full((1, d)), _full(w_dq.shape), _full((1, w_dq.shape[1])), _full(w_uqt.shape),
                _full((MLA_QK, 1))]
    args = [pos, inv_freq, q_pre.reshape(1, d), w_dq.astype(BF16), q_lat.reshape(1, -1), w_uqt,
            q_gain.reshape(-1, 1)]
    out_shape = [jax.ShapeDtypeStruct((MLA_HEADS, QK_PAD, s), BF16)]
    out_specs = [pl.BlockSpec((MLA_HEADS, QK_PAD, rows), lambda i: (0, 0, tile(i)))]
    scratch = [pltpu.VMEM((w_uqt.shape[0], rows), F32)]
    if kv_params is not None:
        kv_pre, w_down, kv_lat, w_up, k_gain = kv_params
        w_dc = w_down[:, :MLA_KV_RANK].astype(BF16)
        w_drt = w_down[:, MLA_KV_RANK:].T.astype(BF16)
        w_upt = w_up.T.astype(BF16)
        in_specs += [_full((1, d)), _full(w_dc.shape), _full(w_drt.shape), _full((1, MLA_KV_RANK)),
                     _full(w_upt.shape), _full((MLA_QK, 1))]
        args += [kv_pre.reshape(1, d), w_dc, w_drt, kv_lat.reshape(1, -1), w_upt,
                 k_gain.reshape(-1, 1)]
        out_shape += [jax.ShapeDtypeStruct((MLA_HEADS, s, QK_PAD), BF16),
                      jax.ShapeDtypeStruct((MLA_HEADS, s // rows, MLA_V, rows), BF16)]
        out_specs += [pl.BlockSpec((MLA_HEADS, rows, QK_PAD), lambda i: (0, tile(i), 0)),
                      pl.BlockSpec((MLA_HEADS, 1, MLA_V, rows), lambda i: (0, tile(i), 0, 0))]
        scratch += [pltpu.VMEM((w_upt.shape[0], rows), F32)]
    return in_specs, args, out_shape, out_specs, scratch


def _mla_qkv(x, pos, inv_freq, q_params, kv_params=None):
    s, d = x.shape
    rows = PROJ_ROWS
    in_specs, args, out_shape, out_specs, scratch = _qkv_operands(
        s, d, rows, lambda i: i, pos, inv_freq, q_params, kv_params)
    kern = functools.partial(_qkv_kernel, rows=rows, with_kv=kv_params is not None)
    return pl.pallas_call(
        kern,
        out_shape=tuple(out_shape),
        grid=(s // rows,),
        in_specs=[pl.BlockSpec((rows, d), lambda i: (i, 0))] + in_specs,
        out_specs=tuple(out_specs),
        scratch_shapes=scratch,
        compiler_params=_params(("parallel",)),
        name="mla_qkv" if kv_params is not None else "mla_q",
    )(x, *args)


def _ffn_qkv(x, norm, w_in, w_out, layer, pos, inv_freq, q_params, kv_params):
    s, d = x.shape
    hidden = w_out.shape[1]
    rows = PROJ_ROWS
    n_tiles = s // rows
    in_specs, args, out_shape, out_specs, scratch = _qkv_operands(
        s, d, rows, lambda i: jnp.maximum(i - 1, 0), pos, inv_freq, q_params, kv_params)
    row_spec = pl.BlockSpec((rows, d), lambda i: (jnp.minimum(i, n_tiles - 1), 0))
    kern = functools.partial(_ffn_qkv_kernel, rows=rows, hidden=hidden)
    return pl.pallas_call(
        kern,
        out_shape=(jax.ShapeDtypeStruct((s, d), F32),) + tuple(out_shape),
        grid=(n_tiles + 1,),
        in_specs=[row_spec, _full((1, d)), _layer_block(w_in, layer),
                  _layer_block(w_out, layer)] + in_specs,
        out_specs=(row_spec,) + tuple(out_specs),
        scratch_shapes=[pltpu.VMEM((rows, d), F32), pltpu.VMEM((rows, hidden), BF16)] + scratch,
        compiler_params=_params(("arbitrary",)),
        name="ffn_qkv",
    )(x, norm.reshape(1, d), w_in, w_out, *args)


def _attn_kernel(bounded_ref, qt_ref, k_ref, vt_ref, o_ref, acc_ref, l_ref, p0_ref, p1_ref, *,
                 tq, tk):
    i = pl.program_id(1)
    heads = qt_ref.shape[0]
    per_q = tq // tk
    assert per_q == 2
    key_c = lax.broadcasted_iota(jnp.int32, (tk, tk), 0) >> 6
    qry_c = lax.broadcasted_iota(jnp.int32, (tk, tk), 1) >> 6
    diag_mask = key_c <= qry_c

    def k_rows(hd, start, size):
        return k_ref[hd, pl.ds(pl.multiple_of(start, size), size), :]

    @pl.when(bounded_ref[0] != 0)
    def _():
        qts = [qt_ref[hd] for hd in range(heads)]

        def sublane_sums(p):
            return p.reshape(p.shape[0] // SUBLANES, SUBLANES, p.shape[1]).sum(axis=0)

        def weighted_values(hd, p_ref, j):
            return _dot(vt_ref[hd, j], p_ref[hd])

        j0, j1 = 2 * i, 2 * i + 1
        for hd in range(heads):
            p = jnp.where(diag_mask, jnp.exp2(_dot(k_rows(hd, j1 * tk, tk), qts[hd][:, tk:])), 0.0)
            p1_ref[hd] = jnp.concatenate([jnp.zeros((tk, tk), F32), p], axis=1).astype(BF16)
            l_ref[hd] = jnp.concatenate([jnp.zeros((SUBLANES, tk), F32), sublane_sums(p)], axis=1)
            acc_ref[hd] = jnp.zeros((MLA_V, tq), F32)

        def tile_pairs(t, n_pairs):
            half = tk // 2
            tot = [None] * heads
            l_tot = [None] * heads
            for u in range(n_pairs):
                prev = 2 * (t + u) - 1
                if u == 0:
                    prev = jnp.where(t == 0, j1, prev)
                stages = ((p0_ref, 2 * (t + u), p1_ref, prev),
                          (p1_ref, 2 * (t + u) + 1, p0_ref, 2 * (t + u)))
                for new_ref, j_new, old_ref, j_old in stages:
                    for hd in range(heads):
                        p_new = jnp.exp2(_dot(k_rows(hd, j_new * tk, tk), qts[hd]))
                        l_part = sublane_sums(p_new)
                        l_tot[hd] = l_part if l_tot[hd] is None else l_tot[hd] + l_part
                        for r0 in (0, half):
                            part = _dot(vt_ref[hd, j_old][:, r0:r0 + half],
                                        old_ref[hd, r0:r0 + half, :])
                            tot[hd] = part if tot[hd] is None else tot[hd] + part
                        new_ref[hd] = p_new.astype(BF16)
            for hd in range(heads):
                acc_ref[hd] += tot[hd]
                l_ref[hd] += l_tot[hd]

        done = 0
        for n_pairs, trips in ((4, i >> 2), (2, (i >> 1) & 1), (1, i & 1)):
            def body(u, carry, n_pairs=n_pairs, done=done):
                tile_pairs(done + n_pairs * u, n_pairs)
                return carry

            lax.fori_loop(0, trips, body, 0)
            done = done + n_pairs * trips

        j_last = jnp.where(i == 0, j1, 2 * i - 1)
        for hd in range(heads):
            p = jnp.exp2(_dot(k_rows(hd, j0 * tk, tk), qts[hd]))
            p = jnp.concatenate([jnp.where(diag_mask, p[:, :tk], 0.0), p[:, tk:]], axis=1)
            acc = (acc_ref[hd] + weighted_values(hd, p1_ref, j_last)
                   + _dot(vt_ref[hd, j0], p.astype(BF16)))
            l = jnp.sum(l_ref[hd] + sublane_sums(p), axis=0, keepdims=True)
            o_ref[hd * MLA_V:(hd + 1) * MLA_V, :] = (acc / l).astype(BF16)

    @pl.when(bounded_ref[0] == 0)
    def _():
        key_w = lax.broadcasted_iota(jnp.int32, (tk, tq), 0) >> 6
        qry_w = lax.broadcasted_iota(jnp.int32, (tk, tq), 1) >> 6
        for hd in range(heads):
            qt = qt_ref[hd]

            def step(j, carry, mask, hd=hd, qt=qt):
                m, l, acc = carry
                s = _dot(k_rows(hd, j * tk, tk), qt)
                if mask is not None:
                    s = jnp.where(mask, s, -jnp.inf)
                m_new = jnp.maximum(m, jnp.max(s, axis=0, keepdims=True))
                alpha = jnp.exp2(m - m_new)
                p = jnp.exp2(s - m_new)
                l = alpha * l + jnp.sum(p, axis=0, keepdims=True)
                acc = alpha * acc + _dot(vt_ref[hd, j], p.astype(BF16))
                return m_new, l, acc

            carry = (jnp.full((1, tq), -jnp.inf, F32), jnp.zeros((1, tq), F32),
                     jnp.zeros((MLA_V, tq), F32))
            carry = lax.fori_loop(0, i * per_q, lambda j, c, step=step: step(j, c, None), carry)
            for jj in range(per_q):
                carry = step(i * per_q + jj, carry, (key_w + jj * (tk // CHUNK)) <= qry_w)
            _, l, acc = carry
            o_ref[hd * MLA_V:(hd + 1) * MLA_V, :] = (acc / l).astype(BF16)


def _mla_attention(bounded, qt, k, vt):
    heads, _, s = qt.shape
    tq, tk, hb = ATTN_TQ, ATTN_TK, ATTN_HEADS_PER_STEP
    kern = functools.partial(_attn_kernel, tq=tq, tk=tk)
    return pl.pallas_call(
        kern,
        out_shape=jax.ShapeDtypeStruct((heads * MLA_V, s), BF16),
        grid_spec=pltpu.PrefetchScalarGridSpec(
            num_scalar_prefetch=1,
            grid=(heads // hb, s // tq),
            in_specs=[
                pl.BlockSpec((hb, QK_PAD, tq), lambda h, i, b: (h, 0, i)),
                pl.BlockSpec((hb, s, QK_PAD), lambda h, i, b: (h, 0, 0)),
                pl.BlockSpec((hb, s // tk, MLA_V, tk), lambda h, i, b: (h, 0, 0, 0)),
            ],
            out_specs=pl.BlockSpec((hb * MLA_V, tq), lambda h, i, b: (h, i)),
            scratch_shapes=[pltpu.VMEM((hb, MLA_V, tq), F32), pltpu.VMEM((hb, SUBLANES, tq), F32),
                            pltpu.VMEM((hb, tk, tq), BF16), pltpu.VMEM((hb, tk, tq), BF16)],
        ),
        compiler_params=_params(("parallel", "arbitrary")),
        name="mla_attn",
    )(bounded, qt, k, vt)


def kernel(x, positions, a_norm, a_w_in, a_w_gate_up, a_b_gate, a_out_norm, a_w_out, b_norm, b_w_dq,
           b_q_latent_norm, b_w_uq, b_q_norm, b_w_out, kv_norm, kv_w_down, kv_latent_norm, kv_w_up,
           k_norm, f_norm, f_w_in, f_w_out):
    batch, s, d = x.shape
    n_a = a_norm.shape[0]
    n_b = b_norm.shape[0]
    half = MLA_ROPE // 2
    inv_freq = (ROPE_THETA ** (-jnp.arange(half, dtype=F32) / half)).reshape(half, 1)
    kv_params = (kv_norm, kv_w_down, kv_latent_norm, kv_w_up, k_norm)
    f_w_in, f_w_out = _to_bf16(f_w_in), _to_bf16(f_w_out)

    def q_params(j):
        return b_norm[j], b_w_dq[j], b_q_latent_norm[j], b_w_uq[j], b_q_norm[j]

    outs = []
    for bi in range(batch):
        xb = x[bi]
        pos = positions[bi].reshape(1, s)
        qt_first = k_sh = vt_sh = None
        for layer in range(n_a + n_b):
            if layer < n_a:
                xb = _gla_layer(xb, a_norm[layer], a_w_in[layer], a_w_gate_up[layer],
                                a_b_gate[layer], a_out_norm[layer], a_w_out[layer])
                if layer == n_a - 1 and n_b > 0:
                    xb, qt_first, k_sh, vt_sh = _ffn_qkv(
                        xb, f_norm[layer], f_w_in, f_w_out, layer, pos, inv_freq,
                        q_params(0), kv_params)
                else:
                    xb = _ffn(xb, f_norm[layer], f_w_in, f_w_out, layer)
            else:
                j = layer - n_a
                qt = qt_first if j == 0 else _mla_qkv(xb, pos, inv_freq, q_params(j))[0]
                logit_bound = (LOGIT_BOUND_COEF * jnp.max(jnp.abs(b_q_norm[j]))
                               * jnp.max(jnp.abs(k_norm)))
                bounded = (logit_bound <= MAX_BOUNDED_LOGIT).astype(jnp.int32).reshape(1)
                attn_t = _mla_attention(bounded, qt, k_sh, vt_sh)
                xb = _ffn(xb, f_norm[layer], f_w_in, f_w_out, layer,
                          attn_t=attn_t, w_o=b_w_out[j])
        outs.append(xb)
    return jnp.stack(outs, axis=0)
```

```python
import functools
import itertools

import jax
import jax.numpy as jnp
from jax import lax
from jax.experimental import pallas as pl
from jax.experimental.pallas import tpu as pltpu

F32 = jnp.float32
BF16 = jnp.bfloat16

EPS = 1e-6
CHUNK = 64
GLA_HEADS = 4
GLA_TAU = 16.0
GLA_GATE_RANK = 16
MLA_HEADS = 16
MLA_NOPE = 64
MLA_ROPE = 32
MLA_QK = MLA_NOPE + MLA_ROPE
MLA_V = 64
MLA_KV_RANK = 256
ROPE_THETA = 10000.0
LOG2_E = 1.4426950408889634

MXU_COLS = 256
LANES = 128
SUBLANES = 8
QK_PAD = LANES
LOGIT_BOUND_COEF = 1.02 * LOG2_E * MLA_QK ** 0.5
MAX_BOUNDED_LOGIT = 100.0
V7X_VMEM_LIMIT_BYTES = 56 * 1024 * 1024

GLA_SUB_ROWS = 256
GLA_ROWS = 2 * GLA_SUB_ROWS
FFN_ROWS = 512
CAST_BLOCK_BYTES = 6 * 1024 * 1024
ATTN_TQ = 1024
ATTN_TK = ATTN_TQ // 2
ATTN_HEADS_PER_STEP = 2
PROJ_ROWS = ATTN_TK


def _dot(a, b):
    return jnp.dot(a, b, preferred_element_type=F32)


def _dot_nt(a, b):
    return lax.dot_general(a, b, (((1,), (1,)), ((), ())), preferred_element_type=F32)


def _dot_tn(a, b):
    return lax.dot_general(a, b, (((0,), (0,)), ((), ())), preferred_element_type=F32)


def _rms_rows(x, g):
    return x * lax.rsqrt(jnp.mean(x * x, axis=-1, keepdims=True) + EPS) * g


def _silu(x):
    return x * jax.nn.sigmoid(x)


def _params(semantics):
    return pltpu.CompilerParams(dimension_semantics=semantics,
                                vmem_limit_bytes=V7X_VMEM_LIMIT_BYTES)


def _alternate(first, second):
    pairs = itertools.zip_longest(first, second)
    return [step for pair in pairs for step in pair if step is not None]


def _interleave(first, second):
    for step in _alternate(first, second):
        step()


def _full(shape):
    return pl.BlockSpec(shape, lambda *_: (0,) * len(shape), pipeline_mode=pl.Buffered(1))


def _layer_block(stack, layer):
    shape = stack.shape[1:]
    return pl.BlockSpec((None,) + shape, lambda *_: (layer,) + (0,) * len(shape),
                        pipeline_mode=pl.Buffered(1))


def _cast_kernel(w_ref, o_ref):
    o_ref[...] = w_ref[...].astype(o_ref.dtype)


def _to_bf16(stack):
    layers, rows, cols = stack.shape
    block_rows = rows
    while block_rows * cols * 4 > CAST_BLOCK_BYTES and block_rows % 32 == 0:
        block_rows //= 2
    block = pl.BlockSpec((None, block_rows, cols), lambda l, i: (l, i, 0))
    return pl.pallas_call(
        _cast_kernel,
        out_shape=jax.ShapeDtypeStruct(stack.shape, BF16),
        grid=(layers, rows // block_rows),
        in_specs=[block],
        out_specs=block,
        compiler_params=_params(("parallel", "parallel")),
        name="to_bf16",
    )(stack)


def _gla_kernel(x_ref, xn_ref, norm_ref, w_main_ref, w_gl_ref, w_gate_ref, b_gate_ref, onorm_ref,
                w_out_ref, out_ref, state_ref, *scratch, sub, dk, dv):
    hk, hv = dk // GLA_HEADS, dv // GLA_HEADS
    subs = [scratch[0:3], scratch[3:6]]

    @pl.when(pl.program_id(0) == 0)
    def _():
        state_ref[...] = jnp.zeros_like(state_ref)

    r_i = lax.broadcasted_iota(jnp.int32, (sub, sub), 0)
    c_i = lax.broadcasted_iota(jnp.int32, (sub, sub), 1)
    tri = ((c_i <= r_i) & ((r_i >> 6) == (c_i >> 6))).astype(BF16)
    qi = lax.broadcasted_iota(jnp.int32, (CHUNK, CHUNK), 0)
    kj = lax.broadcasted_iota(jnp.int32, (CHUNK, CHUNK), 1)
    causal = kj <= qi
    q_scale = hk ** -0.5

    def project_pieces(x_rows, a):
        proj_ref, b_ref, _ = subs[a]
        cell = {}

        def norm():
            cell["h"] = _rms_rows(x_rows[...], norm_ref[...]).astype(BF16)

        def main_block(c0):
            def run():
                proj_ref[:, c0:c0 + MXU_COLS] = _dot(cell["h"], w_main_ref[:, c0:c0 + MXU_COLS])
            return run

        def gate_logits():
            gl = _dot(cell["h"], w_gl_ref[...]).astype(BF16)
            z = _dot(gl, w_gate_ref[...]) + b_gate_ref[...]
            cell["log_a"] = ((jnp.minimum(z, 0.0) - jnp.log1p(jnp.exp(-jnp.abs(z))))
                             * (1.0 / GLA_TAU))

        def cumulate():
            log_a = cell["log_a"]
            hi = log_a.astype(BF16)
            rest = log_a - hi.astype(F32)
            mid = rest.astype(BF16)
            lo = (rest - mid.astype(F32)).astype(BF16)
            b_ref[...] = _dot(tri, hi) + _dot(tri, mid) + _dot(tri, lo)

        blocks = [main_block(c0) for c0 in range(0, 2 * dk + 2 * dv, MXU_COLS)]
        return [norm, gate_logits] + blocks[:2] + [cumulate] + blocks[2:]

    def recur_pieces(a):
        proj_ref, b_ref, o_ref = subs[a]
        cell = {}

        def prepare(c):
            r0 = c * CHUNK
            b_c = b_ref[r0:r0 + CHUNK, :]
            b_last = b_c[CHUNK - 1:CHUNK, :]
            q_c = proj_ref[r0:r0 + CHUNK, 0:dk]
            k_c = proj_ref[r0:r0 + CHUNK, dk:2 * dk]
            cell[c] = dict(
                v=proj_ref[r0:r0 + CHUNK, 2 * dk:2 * dk + dv].astype(BF16),
                q_dec=(q_c * q_scale * jnp.exp(b_c)).astype(BF16),
                k_inv=(k_c * jnp.exp(-b_c)).astype(BF16),
                k_end=(k_c * jnp.exp(b_last - b_c)).astype(BF16),
                decay=jnp.exp(b_last))

        def unit(c, hh):
            def run():
                if hh == 0:
                    prepare(c)
                t = cell[c]
                r0 = c * CHUNK
                ks = slice(hh * hk, (hh + 1) * hk)
                vs = slice(hh * hv, (hh + 1) * hv)
                attn = jnp.where(causal, _dot_nt(t["q_dec"][:, ks], t["k_inv"][:, ks]),
                                 0.0).astype(BF16)
                state_t = state_ref[hh]
                o_h = (_dot(attn, t["v"][:, vs])
                       + _dot_nt(t["q_dec"][:, ks], state_t.astype(BF16)))
                state_ref[hh] = (t["decay"][:, ks] * state_t
                                 + _dot_tn(t["v"][:, vs], t["k_end"][:, ks]))
                o_ref[r0:r0 + CHUNK, vs] = _rms_rows(o_h, onorm_ref[...])
            return run

        return [unit(c, hh) for c in range(sub // CHUNK) for hh in range(GLA_HEADS)]

    def output_pieces(a):
        proj_ref, _, o_ref = subs[a]
        rows = slice(a * sub, (a + 1) * sub)
        cell = {}

        def gate():
            g = _silu(proj_ref[:, 2 * dk + dv:2 * dk + 2 * dv])
            cell["g"] = (o_ref[...] * g).astype(BF16)

        def out_block(c0):
            def run():
                cols = slice(c0, c0 + MXU_COLS)
                out_ref[rows, cols] = x_ref[rows, cols] + _dot(cell["g"], w_out_ref[:, cols])
            return run

        return [gate] + [out_block(c0) for c0 in range(0, out_ref.shape[1], MXU_COLS)]

    @pl.when(pl.program_id(0) == 0)
    def _():
        _interleave(project_pieces(x_ref.at[0:sub, :], 0), [])

    next_proj = project_pieces(xn_ref, 0)
    out0, out1 = output_pieces(0), output_pieces(1)
    split = len(next_proj) - len(out1) - 1
    _interleave(recur_pieces(0), project_pieces(x_ref.at[sub:2 * sub, :], 1))
    _interleave(recur_pieces(1), out0 + next_proj[:split])
    _interleave(out1, next_proj[split:])


def _gla_layer(x, norm, w_in, w_gate_up, b_gate, out_norm, w_out):
    s, d = x.shape
    dk = w_gate_up.shape[1]
    dv = (w_in.shape[1] - 2 * dk - GLA_GATE_RANK) // 2
    n_main = 2 * dk + 2 * dv
    rows = GLA_ROWS
    w_main = w_in.astype(BF16)
    w_gl = jnp.pad(w_in[:, n_main:], ((0, 0), (0, LANES - GLA_GATE_RANK))).astype(BF16)
    w_gate = jnp.pad(w_gate_up, ((0, LANES - GLA_GATE_RANK), (0, 0))).astype(BF16)
    sub = GLA_SUB_ROWS
    last_sub = s // sub - 1
    sub_scratch = [pltpu.VMEM((sub, n_main), F32), pltpu.VMEM((sub, dk), F32),
                   pltpu.VMEM((sub, dv), F32)]
    kern = functools.partial(_gla_kernel, sub=sub, dk=dk, dv=dv)
    return pl.pallas_call(
        kern,
        out_shape=jax.ShapeDtypeStruct((s, d), F32),
        grid=(s // rows,),
        in_specs=[
            pl.BlockSpec((rows, d), lambda i: (i, 0)),
            pl.BlockSpec((sub, d), lambda i: (jnp.minimum(2 * i + 2, last_sub), 0)),
            _full((1, d)),
            _full((d, n_main)),
            _full((d, LANES)),
            _full((LANES, dk)),
            _full((1, dk)),
            _full((1, dv // GLA_HEADS)),
            _full((dv, d)),
        ],
        out_specs=pl.BlockSpec((rows, d), lambda i: (i, 0)),
        scratch_shapes=[pltpu.VMEM((GLA_HEADS, dv // GLA_HEADS, dk // GLA_HEADS), F32)]
        + sub_scratch * 2,
        compiler_params=_params(("arbitrary",)),
        name="gla_layer",
    )(x, x, norm.reshape(1, d), w_main, w_gl, w_gate, b_gate.reshape(1, dk),
      out_norm.reshape(1, -1), w_out.astype(BF16))


def _ffn_body(x, norm_ref, w_in_ref, w_out_ref, out_ref, hidden):
    h = _rms_rows(x, norm_ref[...]).astype(BF16)
    gate = _dot(h, w_in_ref[:, 0:hidden])
    up = _dot(h, w_in_ref[:, hidden:2 * hidden])
    act = (_silu(gate) * up).astype(BF16)
    out_ref[...] = x + _dot(act, w_out_ref[...])


def _ffn_kernel(x_ref, norm_ref, w_in_ref, w_out_ref, out_ref, *, hidden):
    _ffn_body(x_ref[...], norm_ref, w_in_ref, w_out_ref, out_ref, hidden)


def _proj_ffn_kernel(x_ref, ot_ref, w_o_ref, norm_ref, w_in_ref, w_out_ref, out_ref, *, hidden):
    x = x_ref[...] + _dot_tn(ot_ref[...], w_o_ref[...])
    _ffn_body(x, norm_ref, w_in_ref, w_out_ref, out_ref, hidden)


def _ffn_steps(x_ref, norm_ref, w_in_ref, w_out_ref, out_refs, act_ref, hidden):
    cell = {}

    def norm():
        cell["h"] = _rms_rows(x_ref[...], norm_ref[...]).astype(BF16)

    def hidden_block(c0):
        def run():
            gate = _dot(cell["h"], w_in_ref[:, c0:c0 + MXU_COLS])
            up = _dot(cell["h"], w_in_ref[:, hidden + c0:hidden + c0 + MXU_COLS])
            act_ref[:, c0:c0 + MXU_COLS] = (_silu(gate) * up).astype(BF16)
        return run

    def out_block(c0):
        def run():
            cols = slice(c0, c0 + MXU_COLS)
            y = x_ref[:, cols] + _dot(act_ref[...], w_out_ref[:, cols])
            for ref in out_refs:
                ref[:, cols] = y
        return run

    return ([norm] + [hidden_block(c0) for c0 in range(0, hidden, MXU_COLS)]
            + [out_block(c0) for c0 in range(0, x_ref.shape[1], MXU_COLS)])


def _ffn_qkv_kernel(x_ref, norm_ref, w_in_ref, w_out_ref, *rest, rows, hidden):
    qkv_refs, (out_ref, qt_ref, k_ref, vt_ref, xprev_ref, act_ref, qall_ref, kvt_ref) = (
        rest[:-8], rest[-8:])

    @pl.when(pl.program_id(0) == 0)
    def _():
        xprev_ref[...] = jnp.zeros_like(xprev_ref)

    qkv = _qkv_steps((xprev_ref,) + tuple(qkv_refs) + (qt_ref, k_ref, vt_ref, qall_ref, kvt_ref),
                     rows, True)
    ffn = _ffn_steps(x_ref, norm_ref, w_in_ref, w_out_ref, (out_ref, xprev_ref), act_ref, hidden)
    _interleave(qkv, ffn)


def _ffn(x, norm, w_in, w_out, layer, attn_t=None, w_o=None):
    s, d = x.shape
    hidden = w_out.shape[1]
    rows = FFN_ROWS
    row_spec = pl.BlockSpec((rows, d), lambda i: (i, 0))
    ffn_specs = [_full((1, d)), _layer_block(w_in, layer), _layer_block(w_out, layer)]
    ffn_args = (norm.reshape(1, d), w_in, w_out)
    if attn_t is None:
        kern = functools.partial(_ffn_kernel, hidden=hidden)
        in_specs = [row_spec] + ffn_specs
        args = (x,) + ffn_args
        name = "ffn"
    else:
        kern = functools.partial(_proj_ffn_kernel, hidden=hidden)
        in_specs = [row_spec, pl.BlockSpec((attn_t.shape[0], rows), lambda i: (0, i)),
                    _full(w_o.shape)] + ffn_specs
        args = (x, attn_t, w_o.astype(BF16)) + ffn_args
        name = "attn_proj_ffn"
    return pl.pallas_call(
        kern,
        out_shape=jax.ShapeDtypeStruct((s, d), F32),
        grid=(s // rows,),
        in_specs=in_specs,
        out_specs=row_spec,
        compiler_params=_params(("parallel",)),
        name=name,
    )(*args)


def _qkv_steps(refs, rows, with_kv):
    if with_kv:
        (x_ref, pos_ref, inv_freq_ref, qn_ref, w_dq_ref, qln_ref, w_uqt_ref, qg_ref,
         kvn_ref, w_dc_ref, w_drt_ref, lnorm_ref, w_upt_ref, kg_ref,
         qt_ref, k_ref, vt_ref, qall_ref, kvt_ref) = refs
    else:
        (x_ref, pos_ref, inv_freq_ref, qn_ref, w_dq_ref, qln_ref, w_uqt_ref, qg_ref,
         qt_ref, qall_ref) = refs
    half = MLA_ROPE // 2
    group = 4
    pad = jnp.zeros((QK_PAD - MLA_QK, rows), F32)
    cell = {}

    def prologue():
        x = x_ref[...]
        cell["xn"] = x * lax.rsqrt(jnp.mean(x * x, axis=-1, keepdims=True) + EPS)
        ang = pos_ref[...].astype(F32) * inv_freq_ref[...]
        cell["cos"], cell["sin"] = jnp.cos(ang), jnp.sin(ang)

    def q_latent():
        h = (cell["xn"] * qn_ref[...]).astype(BF16)
        cell["c_q"] = _rms_rows(_dot(h, w_dq_ref[...]), qln_ref[...]).astype(BF16)

    def q_heads(h0):
        def run():
            r0, r1 = h0 * MLA_QK, (h0 + group) * MLA_QK
            qall_ref[r0:r1, :] = _dot_nt(w_uqt_ref[r0:r1, :], cell["c_q"])
            g = qg_ref[...]
            cos, sin = cell["cos"], cell["sin"]
            scale = MLA_QK ** -0.5 * LOG2_E
            for hh in range(h0, h0 + group):
                q = qall_ref[hh * MLA_QK:(hh + 1) * MLA_QK, :]
                inv = lax.rsqrt(jnp.mean(q * q, axis=0, keepdims=True) + EPS)
                qn = q * inv * g
                x1, x2 = qn[MLA_NOPE:MLA_NOPE + half, :], qn[MLA_NOPE + half:, :]
                qt = jnp.concatenate([qn[:MLA_NOPE, :], x1 * cos - x2 * sin,
                                      x1 * sin + x2 * cos, pad], axis=0)
                qt_ref[hh] = (qt * scale).astype(BF16)
        return run

    q_steps = [q_latent] + [q_heads(h0) for h0 in range(0, MLA_HEADS, group)]
    if not with_kv:
        return [prologue] + q_steps

    per_head = MLA_NOPE + MLA_V

    def kv_latent():
        h = (cell["xn"] * kvn_ref[...]).astype(BF16)
        cos, sin = cell["cos"], cell["sin"]
        cell["c_kv"] = _rms_rows(_dot(h, w_dc_ref[...]), lnorm_ref[...]).astype(BF16)
        kr = _dot_nt(w_drt_ref[...], h)
        cell["ss_rope"] = jnp.sum(kr * kr, axis=0, keepdims=True)
        krg = kr * kg_ref[MLA_NOPE:, :]
        x1, x2 = krg[:half, :], krg[half:, :]
        cell["r1"] = x1 * cos - x2 * sin
        cell["r2"] = x1 * sin + x2 * cos

    def kv_heads(h0):
        def run():
            r0, r1 = h0 * per_head, (h0 + group) * per_head
            kvt_ref[r0:r1, :] = _dot_nt(w_upt_ref[r0:r1, :], cell["c_kv"])
            g_nope = kg_ref[:MLA_NOPE, :]
            for hh in range(h0, h0 + group):
                kn = kvt_ref[hh * per_head:hh * per_head + MLA_NOPE, :]
                ss = jnp.sum(kn * kn, axis=0, keepdims=True) + cell["ss_rope"]
                inv = lax.rsqrt(ss * (1.0 / MLA_QK) + EPS)
                kt = jnp.concatenate([kn * inv * g_nope, cell["r1"] * inv, cell["r2"] * inv, pad],
                                     axis=0)
                k_ref[hh] = kt.T.astype(BF16)
                vt_ref[hh, 0] = kvt_ref[hh * per_head + MLA_NOPE:(hh + 1) * per_head, :].astype(BF16)
        return run

    kv_steps = [kv_latent] + [kv_heads(h0) for h0 in range(0, MLA_HEADS, group)]
    return [prologue] + _alternate(kv_steps, q_steps)


def _qkv_kernel(*refs, rows, with_kv):
    _interleave(_qkv_steps(refs, rows, with_kv), [])


def _qkv_operands(s, d, rows, tile, pos, inv_freq, q_params, kv_params):
    q_pre, w_dq, q_lat, w_uq, q_gain = q_params
    w_uqt = w_uq.T.astype(BF16)
    in_specs = [pl.BlockSpec((1, rows), lambda i: (0, tile(i))), _full((MLA_ROPE // 2, 1)),
                _full((1, d)), _full(w_dq.shape), _full((1, w_dq.shape[1])), _full(w_uqt.shape),
                _full((MLA_QK, 1))]
    args = [pos, inv_freq, q_pre.reshape(1, d), w_dq.astype(BF16), q_lat.reshape(1, -1), w_uqt,
            q_gain.reshape(-1, 1)]
    out_shape = [jax.ShapeDtypeStruct((MLA_HEADS, QK_PAD, s), BF16)]
    out_specs = [pl.BlockSpec((MLA_HEADS, QK_PAD, rows), lambda i: (0, 0, tile(i)))]
    scratch = [pltpu.VMEM((w_uqt.shape[0], rows), F32)]
    if kv_params is not None:
        kv_pre, w_down, kv_lat, w_up, k_gain = kv_params
        w_dc = w_down[:, :MLA_KV_RANK].astype(BF16)
        w_drt = w_down[:, MLA_KV_RANK:].T.astype(BF16)
        w_upt = w_up.T.astype(BF16)
        in_specs += [_full((1, d)), _full(w_dc.shape), _full(w_drt.shape), _full((1, MLA_KV_RANK)),
                     _full(w_upt.shape), _full((MLA_QK, 1))]
        args += [kv_pre.reshape(1, d), w_dc, w_drt, kv_lat.reshape(1, -1), w_upt,
                 k_gain.reshape(-1, 1)]
        out_shape += [jax.ShapeDtypeStruct((MLA_HEADS, s, QK_PAD), BF16),
                      jax.ShapeDtypeStruct((MLA_HEADS, s // rows, MLA_V, rows), BF16)]
        out_specs += [pl.BlockSpec((MLA_HEADS, rows, QK_PAD), lambda i: (0, tile(i), 0)),
                      pl.BlockSpec((MLA_HEADS, 1, MLA_V, rows), lambda i: (0, tile(i), 0, 0))]
        scratch += [pltpu.VMEM((w_upt.shape[0], rows), F32)]
    return in_specs, args, out_shape, out_specs, scratch


def _mla_qkv(x, pos, inv_freq, q_params, kv_params=None):
    s, d = x.shape
    rows = PROJ_ROWS
    in_specs, args, out_shape, out_specs, scratch = _qkv_operands(
        s, d, rows, lambda i: i, pos, inv_freq, q_params, kv_params)
    kern = functools.partial(_qkv_kernel, rows=rows, with_kv=kv_params is not None)
    return pl.pallas_call(
        kern,
        out_shape=tuple(out_shape),
        grid=(s // rows,),
        in_specs=[pl.BlockSpec((rows, d), lambda i: (i, 0))] + in_specs,
        out_specs=tuple(out_specs),
        scratch_shapes=scratch,
        compiler_params=_params(("parallel",)),
        name="mla_qkv" if kv_params is not None else "mla_q",
    )(x, *args)


def _ffn_qkv(x, norm, w_in, w_out, layer, pos, inv_freq, q_params, kv_params):
    s, d = x.shape
    hidden = w_out.shape[1]
    rows = PROJ_ROWS
    n_tiles = s // rows
    in_specs, args, out_shape, out_specs, scratch = _qkv_operands(
        s, d, rows, lambda i: jnp.maximum(i - 1, 0), pos, inv_freq, q_params, kv_params)
    row_spec = pl.BlockSpec((rows, d), lambda i: (jnp.minimum(i, n_tiles - 1), 0))
    kern = functools.partial(_ffn_qkv_kernel, rows=rows, hidden=hidden)
    return pl.pallas_call(
        kern,
        out_shape=(jax.ShapeDtypeStruct((s, d), F32),) + tuple(out_shape),
        grid=(n_tiles + 1,),
        in_specs=[row_spec, _full((1, d)), _layer_block(w_in, layer),
                  _layer_block(w_out, layer)] + in_specs,
        out_specs=(row_spec,) + tuple(out_specs),
        scratch_shapes=[pltpu.VMEM((rows, d), F32), pltpu.VMEM((rows, hidden), BF16)] + scratch,
        compiler_params=_params(("arbitrary",)),
        name="ffn_qkv",
    )(x, norm.reshape(1, d), w_in, w_out, *args)


def _attn_kernel(bounded_ref, qt_ref, k_ref, vt_ref, o_ref, acc_ref, l_ref, p0_ref, p1_ref, *,
                 tq, tk):
    i = pl.program_id(1)
    heads = qt_ref.shape[0]
    per_q = tq // tk
    assert per_q == 2
    key_c = lax.broadcasted_iota(jnp.int32, (tk, tk), 0) >> 6
    qry_c = lax.broadcasted_iota(jnp.int32, (tk, tk), 1) >> 6
    diag_mask = key_c <= qry_c

    def k_rows(hd, start, size):
        return k_ref[hd, pl.ds(pl.multiple_of(start, size), size), :]

    @pl.when(bounded_ref[0] != 0)
    def _():
        qts = [qt_ref[hd] for hd in range(heads)]

        def sublane_sums(p):
            return p.reshape(p.shape[0] // SUBLANES, SUBLANES, p.shape[1]).sum(axis=0)

        def weighted_values(hd, p_ref, j):
            return _dot(vt_ref[hd, j], p_ref[hd])

        j0, j1 = 2 * i, 2 * i + 1
        for hd in range(heads):
            p = jnp.where(diag_mask, jnp.exp2(_dot(k_rows(hd, j1 * tk, tk), qts[hd][:, tk:])), 0.0)
            p1_ref[hd] = jnp.concatenate([jnp.zeros((tk, tk), F32), p], axis=1).astype(BF16)
            l_ref[hd] = jnp.concatenate([jnp.zeros((SUBLANES, tk), F32), sublane_sums(p)], axis=1)
            acc_ref[hd] = jnp.zeros((MLA_V, tq), F32)

        def tile_pairs(t, n_pairs):
            half = tk // 2
            tot = [None] * heads
            l_tot = [None] * heads
            for u in range(n_pairs):
                prev = 2 * (t + u) - 1
                if u == 0:
                    prev = jnp.where(t == 0, j1, prev)
                stages = ((p0_ref, 2 * (t + u), p1_ref, prev),
                          (p1_ref, 2 * (t + u) + 1, p0_ref, 2 * (t + u)))
                for new_ref, j_new, old_ref, j_old in stages:
                    for hd in range(heads):
                        p_new = jnp.exp2(_dot(k_rows(hd, j_new * tk, tk), qts[hd]))
                        l_part = sublane_sums(p_new)
                        l_tot[hd] = l_part if l_tot[hd] is None else l_tot[hd] + l_part
                        for r0 in (0, half):
                            part = _dot(vt_ref[hd, j_old][:, r0:r0 + half],
                                        old_ref[hd, r0:r0 + half, :])
                            tot[hd] = part if tot[hd] is None else tot[hd] + part
                        new_ref[hd] = p_new.astype(BF16)
            for hd in range(heads):
                acc_ref[hd] += tot[hd]
                l_ref[hd] += l_tot[hd]

        done = 0
        for n_pairs, trips in ((4, i >> 2), (2, (i >> 1) & 1), (1, i & 1)):
            def body(u, carry, n_pairs=n_pairs, done=done):
                tile_pairs(done + n_pairs * u, n_pairs)
                return carry

            lax.fori_loop(0, trips, body, 0)
            done = done + n_pairs * trips

        j_last = jnp.where(i == 0, j1, 2 * i - 1)
        for hd in range(heads):
            p = jnp.exp2(_dot(k_rows(hd, j0 * tk, tk), qts[hd]))
            p = jnp.concatenate([jnp.where(diag_mask, p[:, :tk], 0.0), p[:, tk:]], axis=1)
            acc = (acc_ref[hd] + weighted_values(hd, p1_ref, j_last)
                   + _dot(vt_ref[hd, j0], p.astype(BF16)))
            l = jnp.sum(l_ref[hd] + sublane_sums(p), axis=0, keepdims=True)
            o_ref[hd * MLA_V:(hd + 1) * MLA_V, :] = (acc / l).astype(BF16)

    @pl.when(bounded_ref[0] == 0)
    def _():
        key_w = lax.broadcasted_iota(jnp.int32, (tk, tq), 0) >> 6
        qry_w = lax.broadcasted_iota(jnp.int32, (tk, tq), 1) >> 6
        for hd in range(heads):
            qt = qt_ref[hd]

            def step(j, carry, mask, hd=hd, qt=qt):
                m, l, acc = carry
                s = _dot(k_rows(hd, j * tk, tk), qt)
                if mask is not None:
                    s = jnp.where(mask, s, -jnp.inf)
                m_new = jnp.maximum(m, jnp.max(s, axis=0, keepdims=True))
                alpha = jnp.exp2(m - m_new)
                p = jnp.exp2(s - m_new)
                l = alpha * l + jnp.sum(p, axis=0, keepdims=True)
                acc = alpha * acc + _dot(vt_ref[hd, j], p.astype(BF16))
                return m_new, l, acc

            carry = (jnp.full((1, tq), -jnp.inf, F32), jnp.zeros((1, tq), F32),
                     jnp.zeros((MLA_V, tq), F32))
            carry = lax.fori_loop(0, i * per_q, lambda j, c, step=step: step(j, c, None), carry)
            for jj in range(per_q):
                carry = step(i * per_q + jj, carry, (key_w + jj * (tk // CHUNK)) <= qry_w)
            _, l, acc = carry
            o_ref[hd * MLA_V:(hd + 1) * MLA_V, :] = (acc / l).astype(BF16)


def _mla_attention(bounded, qt, k, vt):
    heads, _, s = qt.shape
    tq, tk, hb = ATTN_TQ, ATTN_TK, ATTN_HEADS_PER_STEP
    kern = functools.partial(_attn_kernel, tq=tq, tk=tk)
    return pl.pallas_call(
        kern,
        out_shape=jax.ShapeDtypeStruct((heads * MLA_V, s), BF16),
        grid_spec=pltpu.PrefetchScalarGridSpec(
            num_scalar_prefetch=1,
            grid=(heads // hb, s // tq),
            in_specs=[
                pl.BlockSpec((hb, QK_PAD, tq), lambda h, i, b: (h, 0, i)),
                pl.BlockSpec((hb, s, QK_PAD), lambda h, i, b: (h, 0, 0)),
                pl.BlockSpec((hb, s // tk, MLA_V, tk), lambda h, i, b: (h, 0, 0, 0)),
            ],
            out_specs=pl.BlockSpec((hb * MLA_V, tq), lambda h, i, b: (h, i)),
            scratch_shapes=[pltpu.VMEM((hb, MLA_V, tq), F32), pltpu.VMEM((hb, SUBLANES, tq), F32),
                            pltpu.VMEM((hb, tk, tq), BF16), pltpu.VMEM((hb, tk, tq), BF16)],
        ),
        compiler_params=_params(("parallel", "arbitrary")),
        name="mla_attn",
    )(bounded, qt, k, vt)


def kernel(x, positions, a_norm, a_w_in, a_w_gate_up, a_b_gate, a_out_norm, a_w_out, b_norm, b_w_dq,
           b_q_latent_norm, b_w_uq, b_q_norm, b_w_out, kv_norm, kv_w_down, kv_latent_norm, kv_w_up,
           k_norm, f_norm, f_w_in, f_w_out):
    batch, s, d = x.shape
    n_a = a_norm.shape[0]
    n_b = b_norm.shape[0]
    half = MLA_ROPE // 2
    inv_freq = (ROPE_THETA ** (-jnp.arange(half, dtype=F32) / half)).reshape(half, 1)
    kv_params = (kv_norm, kv_w_down, kv_latent_norm, kv_w_up, k_norm)
    f_w_in, f_w_out = _to_bf16(f_w_in), _to_bf16(f_w_out)

    def q_params(j):
        return b_norm[j], b_w_dq[j], b_q_latent_norm[j], b_w_uq[j], b_q_norm[j]

    outs = []
    for bi in range(batch):
        xb = x[bi]
        pos = positions[bi].reshape(1, s)
        qt_first = k_sh = vt_sh = None
        for layer in range(n_a + n_b):
            if layer < n_a:
                xb = _gla_layer(xb, a_norm[layer], a_w_in[layer], a_w_gate_up[layer],
                                a_b_gate[layer], a_out_norm[layer], a_w_out[layer])
                if layer == n_a - 1 and n_b > 0:
                    xb, qt_first, k_sh, vt_sh = _ffn_qkv(
                        xb, f_norm[layer], f_w_in, f_w_out, layer, pos, inv_freq,
                        q_params(0), kv_params)
                else:
                    xb = _ffn(xb, f_norm[layer], f_w_in, f_w_out, layer)
            else:
                j = layer - n_a
                qt = qt_first if j == 0 else _mla_qkv(xb, pos, inv_freq, q_params(j))[0]
                logit_bound = (LOGIT_BOUND_COEF * jnp.max(jnp.abs(b_q_norm[j]))
                               * jnp.max(jnp.abs(k_norm)))
                bounded = (logit_bound <= MAX_BOUNDED_LOGIT).astype(jnp.int32).reshape(1)
                attn_t = _mla_attention(bounded, qt, k_sh, vt_sh)
                xb = _ffn(xb, f_norm[layer], f_w_in, f_w_out, layer,
                          attn_t=attn_t, w_o=b_w_out[j])
        outs.append(xb)
    return jnp.stack(outs, axis=0)
```
